```python
import math
import jax
import jax.numpy as jnp
from jax import lax
import numpy as np


D_MODEL = 1024
BATCH = 8
SEQ = 4096
DEPTH = 2

HEAD_DIM = 64
A_HEADS = 8
A_WIDTH = A_HEADS * HEAD_DIM
DILATED_PATTERNS = ((128, 1), (512, 4), (2048, 16))
DILATED_BLOCK = 128
B_HEADS = 4
B_HEAD_DIM = 64
B_QK_WIDTH = B_HEADS * 2 * B_HEAD_DIM
B_V_WIDTH = B_HEADS * 2 * B_HEAD_DIM
HYB_IN_WIDTH = 3 * A_WIDTH + 2 * B_QK_WIDTH + B_V_WIDTH
HYB_MIX_WIDTH = A_WIDTH + B_V_WIDTH
ROPE_THETA = 500000.0
PARTIAL_ROT_DIM = HEAD_DIM // 4
MLA_HEADS = 16
MLA_Q_RANK = 256
MLA_KV_RANK = 128
MLA_NOPE_DIM = 64
MLA_ROPE_DIM = 32
MLA_V_DIM = 64
MLA_ROPE_THETA = 10000.0
MLA_IN_WIDTH = MLA_Q_RANK + MLA_KV_RANK + MLA_ROPE_DIM
MLA_MIX_WIDTH = MLA_HEADS * MLA_V_DIM
FFN_DIM = 2816
CONV_WIDTH = 3
Q_BLOCK = 128
NORM_EPS = 1e-6
MAX_POS_OFFSET = 4096

kernel_name = 'hybrid_dilated_diff_mla_convffn'


def rms_norm(x, g):
    xf = x.astype(jnp.float32)
    y = xf * lax.rsqrt(jnp.mean(xf * xf, axis=-1, keepdims=True) + NORM_EPS)
    return (y * g.astype(jnp.float32)).astype(x.dtype)


def rope_cos_sin(positions, rot_dim, theta):
    inv_freq = theta ** (-jnp.arange(0, rot_dim, 2, dtype=jnp.float32) / rot_dim)
    ang = positions.astype(jnp.float32)[..., None] * inv_freq
    return jnp.cos(ang)[:, :, None, :], jnp.sin(ang)[:, :, None, :]


def apply_rotary(x, cos, sin):
    half = cos.shape[-1]
    r = 2 * half
    xf = x[..., :r].astype(jnp.float32)
    x1, x2 = xf[..., :half], xf[..., half:]
    rot = jnp.concatenate([x1 * cos - x2 * sin, x2 * cos + x1 * sin], axis=-1).astype(x.dtype)
    return jnp.concatenate([rot, x[..., r:]], axis=-1)


def dilated_branch(q, k, v, window, dilation):
    B, S, H, D = q.shape
    blk = DILATED_BLOCK
    n_back = window // dilation
    pad = (-S) % (blk * dilation)
    Sp = S + pad
    L = Sp // dilation
    nb = L // blk

    def to_sub(t):
        t = jnp.pad(t, ((0, 0), (0, pad), (0, 0), (0, 0)))
        return t.reshape(B, L, dilation, H, D).swapaxes(1, 2).reshape(B, dilation, nb, blk, H, D)

    def with_prev(t):
        prev = jnp.pad(t[:, :, :-1], ((0, 0), (0, 0), (1, 0), (0, 0), (0, 0), (0, 0)))
        return jnp.concatenate([prev, t], axis=3)

    qs = to_sub(q)
    kk = with_prev(to_sub(k))
    vv = with_prev(to_sub(v))
    s = jnp.einsum('brnqhd,brnkhd->brnhqk', qs, kk).astype(jnp.float32) * (D ** -0.5)
    qi = jnp.arange(blk)[:, None]
    kj = jnp.arange(2 * blk)[None, :]
    dist = qi + blk - kj
    band = (dist >= 0) & (dist <= n_back)
    has_prev = jnp.arange(nb)[:, None, None] > 0
    valid = band[None] & (has_prev | (kj[None] >= blk))
    s = jnp.where(valid[None, None, :, None], s, -jnp.inf)
    m = jnp.max(s, axis=-1)
    p = jnp.exp(s - m[..., None])
    l = jnp.sum(p, axis=-1)
    o = jnp.einsum('brnhqk,brnkhd->brnqhd', p.astype(v.dtype), vv).astype(jnp.float32)
    o = o / jnp.swapaxes(l, -1, -2)[..., None]

    def from_sub(t):
        rest = t.shape[5:]
        return t.reshape(B, dilation, L, H, *rest).swapaxes(1, 2).reshape(B, Sp, H, *rest)[:, :S]

    return from_sub(o), from_sub(jnp.swapaxes(m, -1, -2)), from_sub(jnp.swapaxes(l, -1, -2))


def dilated_mixture(q, k, v):
    stats = [dilated_branch(q, k, v, w, d) for (w, d) in DILATED_PATTERNS]
    o = jnp.stack([st[0] for st in stats])
    m = jnp.stack([st[1] for st in stats])
    l = jnp.stack([st[2] for st in stats])
    wgt = l * jnp.exp(m - jnp.max(m, axis=0, keepdims=True))
    return jnp.sum(wgt[..., None] * o, axis=0) / jnp.sum(wgt, axis=0)[..., None]


def diff_attention(q1, q2, k1, k2, v, lam):
    B, S, H, D = q1.shape
    nb = S // Q_BLOCK
    scale = D ** -0.5
    kpos = jnp.arange(S)

    def blockify(t):
        return t.reshape(B, nb, Q_BLOCK, H, t.shape[-1]).swapaxes(0, 1)

    def one_block(args):
        i, qa, qb = args
        qpos = i * Q_BLOCK + jnp.arange(Q_BLOCK)
        causal = kpos[None, :] <= qpos[:, None]

        def probs(qx, kx):
            s = jnp.einsum('bqhd,bkhd->bhqk', qx, kx).astype(jnp.float32) * scale
            return jax.nn.softmax(jnp.where(causal, s, -jnp.inf), axis=-1)

        w = probs(qa, k1) - lam * probs(qb, k2)
        return jnp.einsum('bhqk,bkhd->bqhd', w.astype(v.dtype), v)

    o = lax.map(one_block, (jnp.arange(nb), blockify(q1), blockify(q2)))
    return o.swapaxes(0, 1).reshape(B, S, H, v.shape[-1])


def causal_attention(q, k, v):
    B, S, H, D = q.shape
    nb = S // Q_BLOCK
    scale = D ** -0.5
    kpos = jnp.arange(S)
    qb = q.reshape(B, nb, Q_BLOCK, H, D).swapaxes(0, 1)

    def one_block(args):
        i, qi = args
        qpos = i * Q_BLOCK + jnp.arange(Q_BLOCK)
        causal = kpos[None, :] <= qpos[:, None]
        s = jnp.einsum('bqhd,bkhd->bhqk', qi, k).astype(jnp.float32) * scale
        p = jax.nn.softmax(jnp.where(causal, s, -jnp.inf), axis=-1)
        return jnp.einsum('bhqk,bkhd->bqhd', p.astype(v.dtype), v)

    o = lax.map(one_block, (jnp.arange(nb), qb))
    return o.swapaxes(0, 1).reshape(B, S, H, v.shape[-1])


def hybrid_mixer(h, layer_idx, cos, sin, w_in, w_out, lq1, lk1, lq2, lk2, subln):
    B, S, _ = h.shape
    p = h @ w_in
    cuts = [A_WIDTH, 2 * A_WIDTH, 3 * A_WIDTH, 3 * A_WIDTH + B_QK_WIDTH, 3 * A_WIDTH + 2 * B_QK_WIDTH]
    qa, ka, va, qb, kb, vb = jnp.split(p, cuts, axis=-1)
    qa = apply_rotary(qa.reshape(B, S, A_HEADS, HEAD_DIM), cos, sin)
    ka = apply_rotary(ka.reshape(B, S, A_HEADS, HEAD_DIM), cos, sin)
    va = va.reshape(B, S, A_HEADS, HEAD_DIM)
    out_a = dilated_mixture(qa, ka, va).astype(h.dtype).reshape(B, S, A_WIDTH)
    qb = qb.reshape(B, S, B_HEADS, 2, B_HEAD_DIM)
    kb = kb.reshape(B, S, B_HEADS, 2, B_HEAD_DIM)
    q1 = apply_rotary(qb[..., 0, :], cos, sin)
    q2 = apply_rotary(qb[..., 1, :], cos, sin)
    k1 = apply_rotary(kb[..., 0, :], cos, sin)
    k2 = apply_rotary(kb[..., 1, :], cos, sin)
    vb = vb.reshape(B, S, B_HEADS, 2 * B_HEAD_DIM)
    lam_init = 0.8 - 0.6 * math.exp(-0.3 * layer_idx)
    f32 = jnp.float32
    lam = (jnp.exp(jnp.sum(lq1.astype(f32) * lk1.astype(f32)))
           - jnp.exp(jnp.sum(lq2.astype(f32) * lk2.astype(f32))) + lam_init)
    ob = diff_attention(q1, q2, k1, k2, vb, lam)
    ob = (rms_norm(ob, subln) * (1.0 - lam_init)).reshape(B, S, B_V_WIDTH)
    return jnp.concatenate([out_a, ob], axis=-1) @ w_out


def mla_mixer(h, cos, sin, w_in, q_norm, w_uq, kv_norm, w_ukv, w_out):
    B, S, _ = h.shape
    p = h @ w_in
    c_q, c_kv, k_pe = jnp.split(p, [MLA_Q_RANK, MLA_Q_RANK + MLA_KV_RANK], axis=-1)
    q = (rms_norm(c_q, q_norm) @ w_uq).reshape(B, S, MLA_HEADS, MLA_NOPE_DIM + MLA_ROPE_DIM)
    q = jnp.concatenate([q[..., :MLA_NOPE_DIM], apply_rotary(q[..., MLA_NOPE_DIM:], cos, sin)], axis=-1)
    kv = (rms_norm(c_kv, kv_norm) @ w_ukv).reshape(B, S, MLA_HEADS, MLA_NOPE_DIM + MLA_V_DIM)
    k_nope, v = kv[..., :MLA_NOPE_DIM], kv[..., MLA_NOPE_DIM:]
    k_pe = apply_rotary(k_pe[:, :, None, :], cos, sin)
    k = jnp.concatenate([k_nope, jnp.broadcast_to(k_pe, (B, S, MLA_HEADS, MLA_ROPE_DIM))], axis=-1)
    o = causal_attention(q, k, v).reshape(B, S, MLA_MIX_WIDTH)
    return o @ w_out


def conv_ffn(h, w_up, conv_w, conv_b, w_down):
    u = h @ w_up
    u = lax.conv_general_dilated(u, conv_w[:, None, :], window_strides=(1,),
                                 padding=[(CONV_WIDTH - 1, 0)],
                                 dimension_numbers=('NWC', 'WIO', 'NWC'),
                                 feature_group_count=u.shape[-1]) + conv_b
    g, val = jnp.split(u, 2, axis=-1)
    return (jax.nn.silu(g) * val) @ w_down


def setup_inputs(seed: int = 0) -> dict:
    key = jax.random.key(seed)
    ks = jax.random.split(key, 24)
    f32 = jnp.float32
    n_even = (DEPTH + 1) // 2
    n_odd = DEPTH // 2

    def w(k, shape, fan_in):
        return jax.random.normal(k, shape, f32) * (fan_in ** -0.5)

    def gain(k, shape):
        return 1.0 + 0.02 * jax.random.normal(k, shape, f32)

    x = jax.random.normal(ks[0], (BATCH, SEQ, D_MODEL), f32)
    offsets = jax.random.randint(ks[1], (BATCH, 1), 0, MAX_POS_OFFSET, dtype=jnp.int32)
    positions = offsets + jnp.arange(SEQ, dtype=jnp.int32)[None, :]
    return {
        'x': x,
        'positions': positions,
        'attn_norm': gain(ks[2], (DEPTH, D_MODEL)),
        'ffn_norm': gain(ks[3], (DEPTH, D_MODEL)),
        'final_norm': gain(ks[4], (D_MODEL,)),
        'hyb_w_in': w(ks[5], (n_even, D_MODEL, HYB_IN_WIDTH), D_MODEL),
        'hyb_w_out': w(ks[6], (n_even, HYB_MIX_WIDTH, D_MODEL), HYB_MIX_WIDTH),
        'diff_lambda_q1': 0.1 * jax.random.normal(ks[7], (n_even, B_HEAD_DIM), f32),
        'diff_lambda_k1': 0.1 * jax.random.normal(ks[8], (n_even, B_HEAD_DIM), f32),
        'diff_lambda_q2': 0.1 * jax.random.normal(ks[9], (n_even, B_HEAD_DIM), f32),
        'diff_lambda_k2': 0.1 * jax.random.normal(ks[10], (n_even, B_HEAD_DIM), f32),
        'diff_subln': gain(ks[11], (n_even, 2 * B_HEAD_DIM)),
        'mla_w_in': w(ks[12], (n_odd, D_MODEL, MLA_IN_WIDTH), D_MODEL),
        'mla_q_norm': gain(ks[13], (n_odd, MLA_Q_RANK)),
        'mla_w_uq': w(ks[14], (n_odd, MLA_Q_RANK, MLA_HEADS * (MLA_NOPE_DIM + MLA_ROPE_DIM)), MLA_Q_RANK),
        'mla_kv_norm': gain(ks[15], (n_odd, MLA_KV_RANK)),
        'mla_w_ukv': w(ks[16], (n_odd, MLA_KV_RANK, MLA_HEADS * (MLA_NOPE_DIM + MLA_V_DIM)), MLA_KV_RANK),
        'mla_w_out': w(ks[17], (n_odd, MLA_MIX_WIDTH, D_MODEL), MLA_MIX_WIDTH),
        'ffn_w_up': w(ks[18], (DEPTH, D_MODEL, 2 * FFN_DIM), D_MODEL),
        'ffn_conv_w': w(ks[19], (DEPTH, CONV_WIDTH, 2 * FFN_DIM), CONV_WIDTH),
        'ffn_conv_b': 0.01 * jax.random.normal(ks[20], (DEPTH, 2 * FFN_DIM), f32),
        'ffn_w_down': w(ks[21], (DEPTH, FFN_DIM, D_MODEL), FFN_DIM),
    }


def reference(x, positions, attn_norm, ffn_norm, final_norm, hyb_w_in, hyb_w_out,
              diff_lambda_q1, diff_lambda_k1, diff_lambda_q2, diff_lambda_k2, diff_subln,
              mla_w_in, mla_q_norm, mla_w_uq, mla_kv_norm, mla_w_ukv, mla_w_out,
              ffn_w_up, ffn_conv_w, ffn_conv_b, ffn_w_down):
    cos_p, sin_p = rope_cos_sin(positions, PARTIAL_ROT_DIM, ROPE_THETA)
    cos_m, sin_m = rope_cos_sin(positions, MLA_ROPE_DIM, MLA_ROPE_THETA)
    for i in range(DEPTH):
        h = rms_norm(x, attn_norm[i])
        j = i // 2
        if i % 2 == 0:
            x = x + hybrid_mixer(h, i, cos_p, sin_p, hyb_w_in[j], hyb_w_out[j],
                                 diff_lambda_q1[j], diff_lambda_k1[j], diff_lambda_q2[j],
                                 diff_lambda_k2[j], diff_subln[j])
        else:
            x = x + mla_mixer(h, cos_m, sin_m, mla_w_in[j], mla_q_norm[j], mla_w_uq[j],
                              mla_kv_norm[j], mla_w_ukv[j], mla_w_out[j])
        x = x + conv_ffn(rms_norm(x, ffn_norm[i]), ffn_w_up[i], ffn_conv_w[i], ffn_conv_b[i], ffn_w_down[i])
    return rms_norm(x, final_norm)
```

```python
import functools
import math

import jax
import jax.numpy as jnp
from jax import lax
from jax.experimental import pallas as pl
from jax.experimental.pallas import tpu as pltpu

F32 = jnp.float32
BF16 = jnp.bfloat16

D_MODEL = 1024
HEAD_DIM = 64
A_HEADS = 8
A_WIDTH = A_HEADS * HEAD_DIM
DILATED_PATTERNS = ((128, 1), (512, 4), (2048, 16))
B_HEADS = 4
B_HEAD_DIM = 64
B_QK_WIDTH = B_HEADS * 2 * B_HEAD_DIM
B_V_WIDTH = B_HEADS * 2 * B_HEAD_DIM
HYB_IN_WIDTH = 3 * A_WIDTH + 2 * B_QK_WIDTH + B_V_WIDTH
ROPE_THETA = 500000.0
PARTIAL_ROT_DIM = HEAD_DIM // 4
MLA_HEADS = 16
MLA_Q_RANK = 256
MLA_KV_RANK = 128
MLA_NOPE_DIM = 64
MLA_ROPE_DIM = 32
MLA_V_DIM = 64
MLA_ROPE_THETA = 10000.0
MLA_IN_WIDTH = MLA_Q_RANK + MLA_KV_RANK + MLA_ROPE_DIM
FFN_DIM = 2816
CONV_WIDTH = 3
NORM_EPS = 1e-6

LANES = 128
SUBLANES = 8
FFN_TILE = 256
VMEM_LIMIT = 56 * 1024 * 1024


def _rms(x, g):
    ms = jnp.mean(x * x, axis=-1, keepdims=True)
    return (x * lax.rsqrt(ms + NORM_EPS)) * g


def _norm_matmul_kernel(x_ref, g_ref, w_ref, o_ref):
    h = _rms(x_ref[...], g_ref[...]).astype(BF16)
    o_ref[...] = jnp.dot(h, w_ref[...], preferred_element_type=F32).astype(o_ref.dtype)


def _norm_matmul(x, col_block, k_width, g, w, tm, out_dtype=F32):
    m = x.shape[0]
    n = w.shape[1]
    return pl.pallas_call(
        _norm_matmul_kernel,
        grid=(m // tm,),
        in_specs=[
            pl.BlockSpec((tm, k_width), lambda i: (i, col_block)),
            pl.BlockSpec((1, k_width), lambda i: (0, 0)),
            pl.BlockSpec((k_width, n), lambda i: (0, 0)),
        ],
        out_specs=pl.BlockSpec((tm, n), lambda i: (i, 0)),
        out_shape=jax.ShapeDtypeStruct((m, n), out_dtype),
        compiler_params=pltpu.CompilerParams(
            dimension_semantics=("parallel",), vmem_limit_bytes=VMEM_LIMIT),
    )(x, g.reshape(1, k_width), w)


def _matmul_res_kernel(*refs, n_in):
    a_refs = refs[:n_in]
    w_refs = refs[n_in:2 * n_in]
    r_ref, o_ref = refs[2 * n_in:]
    acc = r_ref[...]
    for a_ref, w_ref in zip(a_refs, w_refs):
        acc = acc + jnp.dot(a_ref[...].astype(BF16), w_ref[...], preferred_element_type=F32)
    o_ref[...] = acc


def _matmul_res(a_list, w_list, res, tm):
    m, n = res.shape
    n_in = len(a_list)
    in_specs = [pl.BlockSpec((tm, a.shape[1]), lambda i: (i, 0)) for a in a_list]
    in_specs += [pl.BlockSpec(w.shape, lambda i: (0, 0)) for w in w_list]
    in_specs += [pl.BlockSpec((tm, n), lambda i: (i, 0))]
    return pl.pallas_call(
        functools.partial(_matmul_res_kernel, n_in=n_in),
        grid=(m // tm,),
        in_specs=in_specs,
        out_specs=pl.BlockSpec((tm, n), lambda i: (i, 0)),
        out_shape=jax.ShapeDtypeStruct((m, n), F32),
        compiler_params=pltpu.CompilerParams(
            dimension_semantics=("parallel",), vmem_limit_bytes=VMEM_LIMIT),
    )(*a_list, *w_list, res)


def _ffn_kernel(x_ref, g_ref, wg_ref, wv_ref, cg_ref, cv_ref, wd_ref, fg_ref, o_ref,
                h_scr, acc_scr, ug_buf, uv_buf, carry_g, carry_v, *, tm, nf, final_norm):
    s = pl.program_id(1)
    x = x_ref[...]
    h_scr[...] = _rms(x, g_ref[...]).astype(BF16)

    @pl.when(s == 0)
    def _():
        carry_g[...] = jnp.zeros_like(carry_g)
        carry_v[...] = jnp.zeros_like(carry_v)

    acc_scr[...] = jnp.zeros_like(acc_scr)

    def causal_conv(u, buf, carry, cw, j):
        buf[0:SUBLANES, :] = carry[j]
        buf[SUBLANES:SUBLANES + tm, :] = u
        carry[j] = u[tm - SUBLANES:tm, :]
        u1 = buf[SUBLANES - 1:SUBLANES - 1 + tm, :]
        u2 = buf[SUBLANES - 2:SUBLANES - 2 + tm, :]
        return u2 * cw[0:1, :] + u1 * cw[1:2, :] + u * cw[2:3, :] + cw[3:4, :]

    def fstep(j, c):
        h = h_scr[...]
        ug = jnp.dot(h, wg_ref[j], preferred_element_type=F32)
        uv = jnp.dot(h, wv_ref[j], preferred_element_type=F32)
        gate = causal_conv(ug, ug_buf, carry_g, cg_ref[j], j)
        val = causal_conv(uv, uv_buf, carry_v, cv_ref[j], j)
        act = (gate / (1.0 + jnp.exp(-gate))) * val
        acc_scr[...] += jnp.dot(act.astype(BF16), wd_ref[j], preferred_element_type=F32)
        return c

    lax.fori_loop(0, nf, fstep, 0)
    y = x + acc_scr[...]
    if final_norm:
        y = _rms(y, fg_ref[...])
    o_ref[...] = y


def _ffn(x, g, w_up, conv_w, conv_b, w_down, final_g, tm, final_norm):
    b, s, d = x.shape
    f = w_down.shape[0]
    tf = FFN_TILE
    nf = f // tf
    wg = w_up[:, :f].reshape(d, nf, tf).transpose(1, 0, 2).astype(BF16)
    wv = w_up[:, f:].reshape(d, nf, tf).transpose(1, 0, 2).astype(BF16)
    wd = w_down.reshape(nf, tf, d).astype(BF16)

    def conv_params(lo):
        taps = conv_w[:, lo:lo + f]
        bias = conv_b[lo:lo + f][None, :]
        pad = jnp.zeros((SUBLANES - CONV_WIDTH - 1, f), F32)
        p = jnp.concatenate([taps, bias, pad], axis=0)
        return p.reshape(SUBLANES, nf, tf).transpose(1, 0, 2)

    cg = conv_params(0)
    cv = conv_params(f)
    const3 = lambda bi, si: (0, 0, 0)
    const2 = lambda bi, si: (0, 0)
    return pl.pallas_call(
        functools.partial(_ffn_kernel, tm=tm, nf=nf, final_norm=final_norm),
        grid=(b, s // tm),
        in_specs=[
            pl.BlockSpec((None, tm, d), lambda bi, si: (bi, si, 0)),
            pl.BlockSpec((1, d), const2),
            pl.BlockSpec((nf, d, tf), const3, pipeline_mode=pl.Buffered(1)),
            pl.BlockSpec((nf, d, tf), const3, pipeline_mode=pl.Buffered(1)),
            pl.BlockSpec((nf, SUBLANES, tf), const3),
            pl.BlockSpec((nf, SUBLANES, tf), const3),
            pl.BlockSpec((nf, tf, d), const3, pipeline_mode=pl.Buffered(1)),
            pl.BlockSpec((1, d), const2),
        ],
        out_specs=pl.BlockSpec((None, tm, d), lambda bi, si: (bi, si, 0)),
        out_shape=jax.ShapeDtypeStruct((b, s, d), F32),
        scratch_shapes=[
            pltpu.VMEM((tm, d), BF16),
            pltpu.VMEM((tm, d), F32),
            pltpu.VMEM((tm + SUBLANES, tf), F32),
            pltpu.VMEM((tm + SUBLANES, tf), F32),
            pltpu.VMEM((nf, SUBLANES, tf), F32),
            pltpu.VMEM((nf, SUBLANES, tf), F32),
        ],
        compiler_params=pltpu.CompilerParams(
            dimension_semantics=("parallel", "arbitrary"), vmem_limit_bytes=VMEM_LIMIT),
    )(x, g.reshape(1, d), wg, wv, cg, cv, wd, final_g.reshape(1, d))


def _dilated_count(delta):
    cnt = None
    for window, dil in DILATED_PATTERNS:
        assert dil & (dil - 1) == 0
        n_back = window // dil
        hit = (delta >= 0) & (delta <= n_back * dil) & ((delta & (dil - 1)) == 0)
        c = jnp.where(hit, 1.0, 0.0)
        cnt = c if cnt is None else cnt + c
    return cnt


def _attn_kernel(*refs, split_qk, v_shared, dilated, tq, lookback, lam_init):
    if split_qk:
        q0_ref, q1_ref, k0_ref, k1_ref, v_ref = refs[:5]
        rest = refs[5:]
    else:
        q_ref, k_ref, v_ref = refs[:3]
        q0_ref = q1_ref = q_ref
        rest = refs[3:]
    if v_shared:
        lam_ref, subln_ref, o_ref, acc0, acc1 = rest
    else:
        o_ref, acc0, acc1 = rest

    qi = pl.program_id(2)
    lo = lax.broadcasted_iota(jnp.int32, (1, LANES), 1) < (LANES // 2)
    row = lax.broadcasted_iota(jnp.int32, (tq, tq), 0)
    col = lax.broadcasted_iota(jnp.int32, (tq, tq), 1)
    rel = row - col
    qs = (q0_ref[...], q1_ref[...])
    accs = (acc0, acc1)
    acc0[...] = jnp.zeros_like(acc0)
    acc1[...] = jnp.zeros_like(acc1)

    def block(kj, carry, diagonal):
        start = pl.multiple_of(kj * tq, tq)
        if split_qk:
            ks = (k0_ref[pl.ds(start, tq), :], k1_ref[pl.ds(start, tq), :])
        else:
            kb = k_ref[pl.ds(start, tq), :]
            zk = jnp.zeros_like(kb)
            ks = (jnp.where(lo, kb, zk), jnp.where(lo, zk, kb))
        vb = v_ref[pl.ds(start, tq), :]
        if v_shared:
            vs = (vb, vb)
        else:
            zv = jnp.zeros_like(vb)
            vs = (jnp.where(lo, vb, zv), jnp.where(lo, zv, vb))
        mask = None
        cnt = None
        if dilated:
            cnt = _dilated_count((qi - kj) * tq + rel)
            mask = cnt > 0.0
        elif diagonal:
            mask = rel >= 0
        new = []
        for h in range(2):
            m, l = carry[h]
            s = lax.dot_general(qs[h], ks[h], (((1,), (1,)), ((), ())),
                                preferred_element_type=F32)
            if mask is not None:
                s = jnp.where(mask, s, -jnp.inf)
            m_new = jnp.maximum(m, jnp.max(s, axis=-1, keepdims=True))
            alpha = jnp.exp(m - m_new)
            p = jnp.exp(s - m_new)
            if cnt is not None:
                p = p * cnt
            l_new = alpha * l + jnp.sum(p, axis=-1, keepdims=True)
            accs[h][...] = accs[h][...] * alpha + jnp.dot(
                p.astype(BF16), vs[h], preferred_element_type=F32)
            new.append((m_new, l_new))
        return tuple(new)

    init = tuple((jnp.full((tq, 1), -jnp.inf, F32), jnp.zeros((tq, 1), F32)) for _ in range(2))
    carry = block(qi, init, True)
    first = jnp.maximum(qi - lookback, 0) if dilated else 0
    carry = lax.fori_loop(first, qi, lambda kj, c: block(kj, c, False), carry)
    (m0, l0), (m1, l1) = carry
    o0 = acc0[...] / l0
    o1 = acc1[...] / l1
    if v_shared:
        o = o0 - lam_ref[...] * o1
        o = _rms(o, subln_ref[...]) * (1.0 - lam_init)
    else:
        o = o0 + o1
    o_ref[...] = o.astype(o_ref.dtype)


def _pair_attention(ins, n_pairs, *, split_qk, v_shared, dilated, tq, out_width,
                    lam=None, subln=None, lam_init=0.0):
    b, s, _ = ins[0][0].shape
    nq = s // tq
    lookback = max(-(-w // tq) for w, _ in DILATED_PATTERNS)
    in_specs = []
    for _, is_query, base, step in ins:
        if is_query:
            in_specs.append(pl.BlockSpec(
                (None, tq, LANES),
                lambda bi, pi, qi, base=base, step=step: (bi, qi, base + step * pi)))
        else:
            in_specs.append(pl.BlockSpec(
                (None, s, LANES),
                lambda bi, pi, qi, base=base, step=step: (bi, 0, base + step * pi)))
    args = [a for a, _, _, _ in ins]
    if v_shared:
        in_specs += [pl.BlockSpec((1, LANES), lambda bi, pi, qi: (0, 0))] * 2
        args += [lam, subln]
    return pl.pallas_call(
        functools.partial(_attn_kernel, split_qk=split_qk, v_shared=v_shared, dilated=dilated,
                          tq=tq, lookback=lookback, lam_init=lam_init),
        grid=(b, n_pairs, nq),
        in_specs=in_specs,
        out_specs=pl.BlockSpec((None, tq, LANES), lambda bi, pi, qi: (bi, qi, pi)),
        out_shape=jax.ShapeDtypeStruct((b, s, out_width), BF16),
        scratch_shapes=[pltpu.VMEM((tq, LANES), F32), pltpu.VMEM((tq, LANES), F32)],
        compiler_params=pltpu.CompilerParams(
            dimension_semantics=("parallel", "parallel", "arbitrary"),
            vmem_limit_bytes=VMEM_LIMIT),
    )(*args)


def _rope_tables(positions, rot_dim, theta):
    inv_freq = theta ** (-jnp.arange(0, rot_dim, 2, dtype=F32) / rot_dim)
    ang = positions.astype(F32)[..., None] * inv_freq
    return jnp.cos(ang), jnp.sin(ang)


def _rotate(x, cos, sin):
    half = cos.shape[-1]
    c = cos[:, :, None, :]
    s = sin[:, :, None, :]
    x1, x2 = x[..., :half], x[..., half:2 * half]
    return jnp.concatenate([x1 * c - x2 * s, x2 * c + x1 * s, x[..., 2 * half:]], axis=-1)


def _hybrid_layer(x, layer_idx, cos, sin, g, w_in, w_out, lq1, lk1, lq2, lk2, subln, tm, tq):
    b, s, d = x.shape
    x2 = x.reshape(b * s, d)
    p = _norm_matmul(x2, 0, d, g, w_in.astype(BF16), tm).reshape(b, s, HYB_IN_WIDTH)
    n_rot = (3 * A_WIDTH + 2 * B_QK_WIDTH) // HEAD_DIM
    scale = HEAD_DIM ** -0.5
    qk = p[..., :2 * A_WIDTH].reshape(b, s, 2 * A_HEADS, HEAD_DIM)
    qk = _rotate(qk, cos, sin).reshape(b, s, 2 * A_WIDTH)
    qkb = p[..., 3 * A_WIDTH:3 * A_WIDTH + 2 * B_QK_WIDTH].reshape(b, s, -1, B_HEAD_DIM)
    qkb = _rotate(qkb, cos, sin).reshape(b, s, 2 * B_QK_WIDTH)
    del n_rot
    pr = jnp.concatenate([
        qk[..., :A_WIDTH] * scale, qk[..., A_WIDTH:], p[..., 2 * A_WIDTH:3 * A_WIDTH],
        qkb[..., :B_QK_WIDTH] * (B_HEAD_DIM ** -0.5), qkb[..., B_QK_WIDTH:],
        p[..., 3 * A_WIDTH + 2 * B_QK_WIDTH:]], axis=-1).astype(BF16)
    cb = A_WIDTH // LANES
    out_a = _pair_attention([(pr, True, 0, 1), (pr, False, cb, 1), (pr, False, 2 * cb, 1)],
                            A_HEADS // 2, split_qk=False,
                            v_shared=False, dilated=True, tq=tq, out_width=A_WIDTH)
    lam_init = 0.8 - 0.6 * math.exp(-0.3 * layer_idx)
    lam = (jnp.exp(jnp.sum(lq1 * lk1)) - jnp.exp(jnp.sum(lq2 * lk2)) + lam_init)
    lam_row = jnp.full((1, LANES), lam, F32)
    out_b = _pair_attention([(pr, True, 3 * cb, 1), (pr, False, 4 * cb, 1),
                             (pr, False, 5 * cb, 1)], B_HEADS, split_qk=False,
                            v_shared=True, dilated=False, tq=tq, out_width=B_V_WIDTH,
                            lam=lam_row, subln=subln.reshape(1, LANES), lam_init=lam_init)
    w_out = w_out.astype(BF16)
    y = _matmul_res([out_a.reshape(b * s, A_WIDTH), out_b.reshape(b * s, B_V_WIDTH)],
                    [w_out[:A_WIDTH], w_out[A_WIDTH:]], x2, tm)
    return y.reshape(b, s, d)


def _mla_layer(x, cos, sin, g, w_in, q_norm, w_uq, kv_norm, w_ukv, w_out, tm, tq):
    b, s, d = x.shape
    x2 = x.reshape(b * s, d)
    in_pad = 4 * LANES
    w_in_p = jnp.pad(w_in, ((0, 0), (0, in_pad - MLA_IN_WIDTH))).astype(BF16)
    p = _norm_matmul(x2, 0, d, g, w_in_p, tm)
    qd = MLA_NOPE_DIM + MLA_ROPE_DIM
    w_uq_p = jnp.pad(w_uq.reshape(MLA_Q_RANK, MLA_HEADS, qd),
                     ((0, 0), (0, 0), (0, LANES - qd))).reshape(MLA_Q_RANK, MLA_HEADS * LANES)
    q = _norm_matmul(p, 0, MLA_Q_RANK, q_norm, w_uq_p.astype(BF16), tm)
    w_kv = w_ukv.reshape(MLA_KV_RANK, MLA_HEADS, MLA_NOPE_DIM + MLA_V_DIM)
    w_k = jnp.pad(w_kv[:, :, :MLA_NOPE_DIM], ((0, 0), (0, 0), (0, LANES - MLA_NOPE_DIM)))
    w_k = w_k.reshape(MLA_KV_RANK, MLA_HEADS * LANES)
    w_v = w_kv[:, :, MLA_NOPE_DIM:].reshape(MLA_KV_RANK, MLA_HEADS * MLA_V_DIM)
    w_kv_p = jnp.concatenate([w_k, w_v], axis=1).astype(BF16)
    kv = _norm_matmul(p, MLA_Q_RANK // MLA_KV_RANK, MLA_KV_RANK, kv_norm, w_kv_p, tm)
    scale = qd ** -0.5
    q4 = q.reshape(b, s, MLA_HEADS, LANES)
    q_rot = _rotate(q4[..., MLA_NOPE_DIM:], cos, sin)
    q4 = jnp.concatenate([q4[..., :MLA_NOPE_DIM], q_rot], axis=-1) * scale
    qf = q4.reshape(b, s, MLA_HEADS * LANES).astype(BF16)
    k_pe = p[:, MLA_Q_RANK + MLA_KV_RANK:MLA_IN_WIDTH].reshape(b, s, 1, MLA_ROPE_DIM)
    k_pe = _rotate(k_pe, cos, sin)
    k_pe = jnp.pad(k_pe, ((0, 0), (0, 0), (0, 0),
                          (MLA_NOPE_DIM, LANES - MLA_NOPE_DIM - MLA_ROPE_DIM)))
    kw = MLA_HEADS * LANES
    kf = (kv[:, :kw].reshape(b, s, MLA_HEADS, LANES) + k_pe).reshape(b, s, kw).astype(BF16)
    vf = kv[:, kw:].reshape(b, s, MLA_HEADS * MLA_V_DIM).astype(BF16)
    o = _pair_attention([(qf, True, 0, 2), (qf, True, 1, 2), (kf, False, 0, 2),
                         (kf, False, 1, 2), (vf, False, 0, 1)], MLA_HEADS // 2, split_qk=True,
                        v_shared=False, dilated=False, tq=tq, out_width=MLA_HEADS * MLA_V_DIM)
    y = _matmul_res([o.reshape(b * s, MLA_HEADS * MLA_V_DIM)], [w_out.astype(BF16)], x2, tm)
    return y.reshape(b, s, d)


def _choose_tiles(s):
    tq = min(512, s)
    tm = min(512, s)
    return tm, tq


def kernel(x, positions, attn_norm, ffn_norm, final_norm, hyb_w_in, hyb_w_out, diff_lambda_q1, diff_lambda_k1, diff_lambda_q2, diff_lambda_k2, diff_subln, mla_w_in, mla_q_norm, mla_w_uq, mla_kv_norm, mla_w_ukv, mla_w_out, ffn_w_up, ffn_conv_w, ffn_conv_b, ffn_w_down):
    depth = attn_norm.shape[0]
    tm, tq = _choose_tiles(x.shape[1])
    cos_p, sin_p = _rope_tables(positions, PARTIAL_ROT_DIM, ROPE_THETA)
    cos_m, sin_m = _rope_tables(positions, MLA_ROPE_DIM, MLA_ROPE_THETA)
    for i in range(depth):
        j = i // 2
        if i % 2 == 0:
            x = _hybrid_layer(x, i, cos_p, sin_p, attn_norm[i], hyb_w_in[j], hyb_w_out[j],
                              diff_lambda_q1[j], diff_lambda_k1[j], diff_lambda_q2[j],
                              diff_lambda_k2[j], diff_subln[j], tm, tq)
        else:
            x = _mla_layer(x, cos_m, sin_m, attn_norm[i], mla_w_in[j], mla_q_norm[j], mla_w_uq[j],
                           mla_kv_norm[j], mla_w_ukv[j], mla_w_out[j], tm, tq)
        x = _ffn(x, ffn_norm[i], ffn_w_up[i], ffn_conv_w[i], ffn_conv_b[i], ffn_w_down[i],
                 final_norm, tm, final_norm=(i == depth - 1))
    return x
```

```python
import functools
import math

import jax
import jax.numpy as jnp
from jax import lax
from jax.experimental import pallas as pl
from jax.experimental.pallas import tpu as pltpu

F32 = jnp.float32
BF16 = jnp.bfloat16

D_MODEL = 1024
HEAD_DIM = 64
A_HEADS = 8
A_WIDTH = A_HEADS * HEAD_DIM
DILATED_PATTERNS = ((128, 1), (512, 4), (2048, 16))
B_HEADS = 4
B_HEAD_DIM = 64
B_QK_WIDTH = B_HEADS * 2 * B_HEAD_DIM
B_V_WIDTH = B_HEADS * 2 * B_HEAD_DIM
HYB_IN_WIDTH = 3 * A_WIDTH + 2 * B_QK_WIDTH + B_V_WIDTH
ROPE_THETA = 500000.0
PARTIAL_ROT_DIM = HEAD_DIM // 4
MLA_HEADS = 16
MLA_Q_RANK = 256
MLA_KV_RANK = 128
MLA_NOPE_DIM = 64
MLA_ROPE_DIM = 32
MLA_V_DIM = 64
MLA_ROPE_THETA = 10000.0
MLA_IN_WIDTH = MLA_Q_RANK + MLA_KV_RANK + MLA_ROPE_DIM
FFN_DIM = 2816
CONV_WIDTH = 3
NORM_EPS = 1e-6
LOG2E = 1.4426950408889634

LANES = 128
HALF = LANES // 2
SUBLANES = 8
FFN_TILE = 256
VMEM_LIMIT = 56 * 1024 * 1024
NT_DIMS = (((1,), (1,)), ((), ()))


def _rms(x, g):
    ms = jnp.mean(x * x, axis=-1, keepdims=True)
    return (x * lax.rsqrt(ms + NORM_EPS)) * g


def _lo_mask():
    return lax.broadcasted_iota(jnp.int32, (1, LANES), 1) < HALF


def _rope_tables(positions, rot_dim, theta, period, lane0):
    half = rot_dim // 2
    inv_freq = theta ** (-jnp.arange(0, rot_dim, 2, dtype=F32) / rot_dim)
    ang = positions.astype(F32).reshape(-1, 1) * inv_freq
    cos, sin = jnp.cos(ang), jnp.sin(ang)
    t = ang.shape[0]
    one = lambda n: jnp.ones((t, n), F32)
    zero = lambda n: jnp.zeros((t, n), F32)
    tail = period - lane0 - rot_dim
    c = jnp.concatenate([one(lane0), cos, cos, one(tail)], axis=1)
    sa = jnp.concatenate([zero(lane0 + half), sin, zero(tail)], axis=1)
    sb = jnp.concatenate([zero(lane0), -sin, zero(half + tail)], axis=1)
    rep = LANES // period
    return tuple(jnp.tile(a, (1, rep)) for a in (c, sa, sb))


def _rope(x, c, sa, sb, half):
    return x * c + pltpu.roll(x, half, 1) * sa + pltpu.roll(x, LANES - half, 1) * sb


def _hyb_proj_kernel(x_ref, g_ref, w_ref, c_ref, sa_ref, sb_ref, o_ref, *, group, rot, qscale):
    h = _rms(x_ref[...], g_ref[...]).astype(BF16)
    c, sa, sb = c_ref[...], sa_ref[...], sb_ref[...]
    half = PARTIAL_ROT_DIM // 2
    for g0 in range(0, o_ref.shape[1], group):
        y = jnp.dot(h, w_ref[:, g0:g0 + group], preferred_element_type=F32)
        for off in range(0, group, LANES):
            blk = y[:, off:off + LANES]
            kind = rot[(g0 + off) // LANES]
            if kind:
                blk = _rope(blk, c, sa, sb, half)
            if kind == 2:
                blk = blk * qscale
            o_ref[:, g0 + off:g0 + off + LANES] = blk.astype(BF16)


def _hyb_proj(x2, g, w, tables, tm):
    m, d = x2.shape
    n = w.shape[1]
    cb = A_WIDTH // LANES
    rot = (2,) * cb + (1,) * cb + (0,) * cb + (2,) * cb + (1,) * cb + (0,) * cb
    row = lambda i: (i, 0)
    const = lambda i: (0, 0)
    return pl.pallas_call(
        functools.partial(_hyb_proj_kernel, group=4 * LANES, rot=rot,
                          qscale=HEAD_DIM ** -0.5 * LOG2E),
        grid=(m // tm,),
        in_specs=[pl.BlockSpec((tm, d), row), pl.BlockSpec((1, d), const),
                  pl.BlockSpec((d, n), const)] + [pl.BlockSpec((tm, LANES), row)] * 3,
        out_specs=pl.BlockSpec((tm, n), row),
        out_shape=jax.ShapeDtypeStruct((m, n), BF16),
        compiler_params=pltpu.CompilerParams(
            dimension_semantics=("parallel",), vmem_limit_bytes=VMEM_LIMIT),
    )(x2, g.reshape(1, d), w, *tables)


def _mla_proj_kernel(x_ref, g_ref, w_in_ref, qn_ref, w_uq_ref, kvn_ref, w_kv_ref,
                     c_ref, sa_ref, sb_ref, q_out, k_out, v_out, *, qscale):
    h = _rms(x_ref[...], g_ref[...]).astype(BF16)
    p = jnp.dot(h, w_in_ref[...], preferred_element_type=F32)
    c, sa, sb = c_ref[...], sa_ref[...], sb_ref[...]
    half = MLA_ROPE_DIM // 2
    cq = _rms(p[:, :MLA_Q_RANK], qn_ref[...]).astype(BF16)
    q = jnp.dot(cq, w_uq_ref[...], preferred_element_type=F32)
    for hd in range(MLA_HEADS):
        blk = q[:, hd * LANES:(hd + 1) * LANES]
        q_out[:, hd * LANES:(hd + 1) * LANES] = (_rope(blk, c, sa, sb, half) * qscale).astype(BF16)
    ckv = _rms(p[:, MLA_Q_RANK:MLA_Q_RANK + MLA_KV_RANK], kvn_ref[...]).astype(BF16)
    kv = jnp.dot(ckv, w_kv_ref[...], preferred_element_type=F32)
    pe = _rope(pltpu.roll(p[:, MLA_Q_RANK + MLA_KV_RANK:], MLA_NOPE_DIM, 1), c, sa, sb, half)
    for hd in range(MLA_HEADS):
        k_out[:, hd * LANES:(hd + 1) * LANES] = (kv[:, hd * LANES:(hd + 1) * LANES] + pe).astype(BF16)
    v_out[...] = kv[:, MLA_HEADS * LANES:].astype(BF16)


def _mla_proj(x2, g, w_in, q_norm, w_uq, kv_norm, w_ukv, tables, tm):
    m, d = x2.shape
    in_pad = MLA_Q_RANK + MLA_KV_RANK + LANES
    w_in_p = jnp.pad(w_in, ((0, 0), (0, in_pad - MLA_IN_WIDTH))).astype(BF16)
    qd = MLA_NOPE_DIM + MLA_ROPE_DIM
    w_uq_p = jnp.pad(w_uq.reshape(MLA_Q_RANK, MLA_HEADS, qd),
                     ((0, 0), (0, 0), (0, LANES - qd))).reshape(MLA_Q_RANK, MLA_HEADS * LANES)
    w_kv = w_ukv.reshape(MLA_KV_RANK, MLA_HEADS, MLA_NOPE_DIM + MLA_V_DIM)
    w_k = jnp.pad(w_kv[:, :, :MLA_NOPE_DIM], ((0, 0), (0, 0), (0, LANES - MLA_NOPE_DIM)))
    w_k = w_k.reshape(MLA_KV_RANK, MLA_HEADS * LANES)
    w_v = w_kv[:, :, MLA_NOPE_DIM:].reshape(MLA_KV_RANK, MLA_HEADS * MLA_V_DIM)
    w_kv_p = jnp.concatenate([w_k, w_v], axis=1).astype(BF16)
    qw = MLA_HEADS * LANES
    vw = MLA_HEADS * MLA_V_DIM
    row = lambda i: (i, 0)
    const = lambda i: (0, 0)
    full = lambda a: pl.BlockSpec(a.shape, const)
    args = (x2, g.reshape(1, d), w_in_p, q_norm.reshape(1, -1), w_uq_p.astype(BF16),
            kv_norm.reshape(1, -1), w_kv_p)
    return pl.pallas_call(
        functools.partial(_mla_proj_kernel, qscale=qd ** -0.5 * LOG2E),
        grid=(m // tm,),
        in_specs=[pl.BlockSpec((tm, d), row)] + [full(a) for a in args[1:]]
        + [pl.BlockSpec((tm, LANES), row)] * 3,
        out_specs=[pl.BlockSpec((tm, qw), row), pl.BlockSpec((tm, qw), row),
                   pl.BlockSpec((tm, vw), row)],
        out_shape=[jax.ShapeDtypeStruct((m, qw), BF16), jax.ShapeDtypeStruct((m, qw), BF16),
                   jax.ShapeDtypeStruct((m, vw), BF16)],
        compiler_params=pltpu.CompilerParams(
            dimension_semantics=("parallel",), vmem_limit_bytes=VMEM_LIMIT),
    )(*args, *tables)


def _matmul_res_kernel(*refs, n_in):
    a_refs = refs[:n_in]
    w_refs = refs[n_in:2 * n_in]
    r_ref, o_ref = refs[2 * n_in:]
    acc = r_ref[...]
    for a_ref, w_ref in zip(a_refs, w_refs):
        acc = acc + jnp.dot(a_ref[...], w_ref[...], preferred_element_type=F32)
    o_ref[...] = acc


def _matmul_res(a_list, w_list, res, tm):
    m, n = res.shape
    n_in = len(a_list)
    in_specs = [pl.BlockSpec((tm, a.shape[1]), lambda i: (i, 0)) for a in a_list]
    in_specs += [pl.BlockSpec(w.shape, lambda i: (0, 0)) for w in w_list]
    in_specs += [pl.BlockSpec((tm, n), lambda i: (i, 0))]
    return pl.pallas_call(
        functools.partial(_matmul_res_kernel, n_in=n_in),
        grid=(m // tm,),
        in_specs=in_specs,
        out_specs=pl.BlockSpec((tm, n), lambda i: (i, 0)),
        out_shape=jax.ShapeDtypeStruct((m, n), F32),
        compiler_params=pltpu.CompilerParams(
            dimension_semantics=("parallel",), vmem_limit_bytes=VMEM_LIMIT),
    )(*a_list, *w_list, res)


def _ffn_kernel(x_ref, g_ref, wg_ref, wv_ref, cg_ref, cv_ref, wd_ref, fg_ref, o_ref,
                h_scr, acc_scr, ubuf_a, ubuf_b, carry, *, tm, nf, ns, final_norm):
    s = pl.program_id(1)
    x = x_ref[...]
    h_scr[...] = _rms(x, g_ref[...]).astype(BF16)

    @pl.when(s == 0)
    def _():
        carry[...] = jnp.zeros_like(carry)

    acc_scr[...] = jnp.zeros_like(acc_scr)

    def up(j, ubuf):
        h = h_scr[...]
        for part, w_ref in enumerate((wg_ref, wv_ref)):
            u = jnp.dot(h, w_ref[j], preferred_element_type=F32)
            for c in range(ns):
                idx = part * ns + c
                col = u[:, c * LANES:(c + 1) * LANES]
                ubuf[idx, 0:SUBLANES, :] = carry[j, idx]
                ubuf[idx, SUBLANES:SUBLANES + tm, :] = col
                carry[j, idx] = col[tm - SUBLANES:tm, :]

    def conv(ubuf, idx, cw):
        u0 = ubuf[idx, SUBLANES:SUBLANES + tm, :]
        u1 = ubuf[idx, SUBLANES - 1:SUBLANES - 1 + tm, :]
        u2 = ubuf[idx, SUBLANES - 2:SUBLANES - 2 + tm, :]
        return u2 * cw[0:1, :] + u1 * cw[1:2, :] + u0 * cw[2:3, :] + cw[3:4, :]

    def down(j, ubuf):
        cg = cg_ref[j]
        cv = cv_ref[j]
        parts = []
        for c in range(ns):
            gate = conv(ubuf, c, cg[:, c * LANES:(c + 1) * LANES])
            val = conv(ubuf, ns + c, cv[:, c * LANES:(c + 1) * LANES])
            parts.append(((gate / (1.0 + jnp.exp(-gate))) * val).astype(BF16))
        act = jnp.concatenate(parts, axis=1)
        acc_scr[...] += jnp.dot(act, wd_ref[j], preferred_element_type=F32)

    assert nf % 2 == 1
    up(0, ubuf_a)

    def body(jj, c):
        j = 2 * jj
        up(j + 1, ubuf_b)
        down(j, ubuf_a)
        up(j + 2, ubuf_a)
        down(j + 1, ubuf_b)
        return c

    lax.fori_loop(0, nf // 2, body, 0)
    down(nf - 1, ubuf_a)
    y = x + acc_scr[...]
    if final_norm:
        y = _rms(y, fg_ref[...])
    o_ref[...] = y


def _ffn(x, g, w_up, conv_w, conv_b, w_down, final_g, tm, final_norm):
    b, s, d = x.shape
    f = w_down.shape[0]
    tf = FFN_TILE
    nf = f // tf
    ns = tf // LANES
    wg = w_up[:, :f].reshape(d, nf, tf).transpose(1, 0, 2).astype(BF16)
    wv = w_up[:, f:].reshape(d, nf, tf).transpose(1, 0, 2).astype(BF16)
    wd = w_down.reshape(nf, tf, d).astype(BF16)

    def conv_params(lo):
        taps = conv_w[:, lo:lo + f]
        bias = conv_b[lo:lo + f][None, :]
        pad = jnp.zeros((SUBLANES - CONV_WIDTH - 1, f), F32)
        p = jnp.concatenate([taps, bias, pad], axis=0)
        return p.reshape(SUBLANES, nf, tf).transpose(1, 0, 2)

    cg = conv_params(0)
    cv = conv_params(f)
    const3 = lambda bi, si: (0, 0, 0)
    const2 = lambda bi, si: (0, 0)
    return pl.pallas_call(
        functools.partial(_ffn_kernel, tm=tm, nf=nf, ns=ns, final_norm=final_norm),
        grid=(b, s // tm),
        in_specs=[
            pl.BlockSpec((None, tm, d), lambda bi, si: (bi, si, 0)),
            pl.BlockSpec((1, d), const2),
            pl.BlockSpec((nf, d, tf), const3, pipeline_mode=pl.Buffered(1)),
            pl.BlockSpec((nf, d, tf), const3, pipeline_mode=pl.Buffered(1)),
            pl.BlockSpec((nf, SUBLANES, tf), const3),
            pl.BlockSpec((nf, SUBLANES, tf), const3),
            pl.BlockSpec((nf, tf, d), const3, pipeline_mode=pl.Buffered(1)),
            pl.BlockSpec((1, d), const2),
        ],
        out_specs=pl.BlockSpec((None, tm, d), lambda bi, si: (bi, si, 0)),
        out_shape=jax.ShapeDtypeStruct((b, s, d), F32),
        scratch_shapes=[
            pltpu.VMEM((tm, d), BF16),
            pltpu.VMEM((tm, d), F32),
            pltpu.VMEM((2 * ns, tm + SUBLANES, LANES), F32),
            pltpu.VMEM((2 * ns, tm + SUBLANES, LANES), F32),
            pltpu.VMEM((nf, 2 * ns, SUBLANES, LANES), F32),
        ],
        compiler_params=pltpu.CompilerParams(
            dimension_semantics=("parallel", "arbitrary"), vmem_limit_bytes=VMEM_LIMIT),
    )(x, g.reshape(1, d), wg, wv, cg, cv, wd, final_g.reshape(1, d))


def _attn_kernel(*refs, split_qk, v_shared, tq, tk, lam_init):
    if split_qk:
        q0_ref, q1_ref, k0_ref, k1_ref, v_ref = refs[:5]
        rest = refs[5:]
    else:
        q0_ref, k0_ref, v_ref = refs[:3]
        q1_ref, k1_ref = q0_ref, k0_ref
        rest = refs[3:]
    if v_shared:
        lam_ref, subln_ref, o_ref, acc0, acc1, mx0, mx1 = rest
    else:
        o_ref, acc0, acc1, mx0, mx1 = rest
    q_refs, k_refs, accs, mxs = (q0_ref, q1_ref), (k0_ref, k1_ref), (acc0, acc1), (mx0, mx1)
    acc_rep = acc0.shape[1] // LANES
    qi = pl.program_id(2)
    lo = _lo_mask()
    nkb = tq // tk

    def k_of(h, start):
        kb = k_refs[h][pl.ds(start, tk), :]
        if split_qk:
            return kb
        z = jnp.zeros_like(kb)
        return jnp.where(lo, kb, z) if h == 0 else jnp.where(lo, z, kb)

    def v_of(h, start):
        vb = v_ref[pl.ds(start, tk), :]
        ones = jnp.ones_like(vb)
        if v_shared:
            return jnp.concatenate([vb, ones], axis=1)
        return jnp.where(lo, vb, ones) if h == 0 else jnp.where(lo, ones, vb)

    def step(h, kj, r0, masked):
        start = pl.multiple_of(kj * tk, tk)
        rows = tq - r0
        s = lax.dot_general(q_refs[h][r0:tq, :], k_of(h, start), NT_DIMS,
                            preferred_element_type=F32)
        if masked:
            row = lax.broadcasted_iota(jnp.int32, (rows, tk), 0)
            col = lax.broadcasted_iota(jnp.int32, (rows, tk), 1)
            s = jnp.where(row >= col, s, -jnp.inf)
        m_old = mxs[h][r0:tq, :]
        m_new = jnp.maximum(m_old, jnp.max(s, axis=-1, keepdims=True))
        alpha = jnp.exp2(m_old - m_new)
        p = jnp.exp2(s - jnp.concatenate([m_new] * (tk // LANES), axis=1))
        if acc_rep > 1:
            alpha = jnp.concatenate([alpha] * acc_rep, axis=1)
        accs[h][r0:tq, :] = accs[h][r0:tq, :] * alpha + jnp.dot(
            p.astype(BF16), v_of(h, start), preferred_element_type=F32)
        mxs[h][r0:tq, :] = m_new

    acc0[...] = jnp.zeros_like(acc0)
    acc1[...] = jnp.zeros_like(acc1)
    mx0[...] = jnp.full(mx0.shape, -jnp.inf, F32)
    mx1[...] = jnp.full(mx1.shape, -jnp.inf, F32)
    for d in range(nkb):
        for h in range(2):
            step(h, qi * nkb + d, d * tk, True)

    def body(kj, c):
        for h in range(2):
            step(h, kj, 0, False)
        return c

    lax.fori_loop(0, qi * nkb, body, 0)
    a0 = acc0[...]
    a1 = acc1[...]
    if v_shared:
        o = a0[:, :LANES] / a0[:, LANES:] - lam_ref[...] * (a1[:, :LANES] / a1[:, LANES:])
        o = _rms(o, subln_ref[...]) * (1.0 - lam_init)
    else:
        o = jnp.where(lo, a0 / pltpu.roll(a0, HALF, 1), a1 / pltpu.roll(a1, HALF, 1))
    o_ref[...] = o.astype(o_ref.dtype)


def _pair_attention(ins, n_pairs, *, split_qk, v_shared, tq, tk, out_width,
                    lam=None, subln=None, lam_init=0.0):
    b, s, _ = ins[0][0].shape
    in_specs = []
    for _, is_query, base, step in ins:
        if is_query:
            in_specs.append(pl.BlockSpec(
                (None, tq, LANES),
                lambda bi, pi, qi, base=base, step=step: (bi, qi, base + step * pi)))
        else:
            in_specs.append(pl.BlockSpec(
                (None, s, LANES),
                lambda bi, pi, qi, base=base, step=step: (bi, 0, base + step * pi)))
    args = [a for a, _, _, _ in ins]
    if v_shared:
        in_specs += [pl.BlockSpec((1, LANES), lambda bi, pi, qi: (0, 0))] * 2
        args += [lam, subln]
    acc_w = 2 * LANES if v_shared else LANES
    return pl.pallas_call(
        functools.partial(_attn_kernel, split_qk=split_qk, v_shared=v_shared, tq=tq, tk=tk,
                          lam_init=lam_init),
        grid=(b, n_pairs, s // tq),
        in_specs=in_specs,
        out_specs=pl.BlockSpec((None, tq, LANES), lambda bi, pi, qi: (bi, qi, pi)),
        out_shape=jax.ShapeDtypeStruct((b, s, out_width), BF16),
        scratch_shapes=[pltpu.VMEM((tq, acc_w), F32), pltpu.VMEM((tq, acc_w), F32),
                        pltpu.VMEM((tq, LANES), F32), pltpu.VMEM((tq, LANES), F32)],
        compiler_params=pltpu.CompilerParams(
            dimension_semantics=("parallel", "parallel", "arbitrary"),
            vmem_limit_bytes=VMEM_LIMIT),
    )(*args)


def _count_bias(delta, patterns):
    cnt = jnp.zeros(delta.shape, F32)
    for window, dil in patterns:
        n_back = window // dil
        hit = (delta >= 0) & (delta <= n_back * dil) & (delta % dil == 0)
        cnt = cnt + hit.astype(F32)
    return jnp.where(cnt > 0, jnp.log2(jnp.maximum(cnt, 1.0)), -jnp.inf)


def _dilated_kernel(q_ref, k_ref, v_ref, bfull_ref, bdiag_ref, bsub_ref, o_ref,
                    acc0, acc1, m0s, m1s, qf, kf, vf, *, tq, dil, sub):
    s_len = q_ref.shape[0]
    lo = _lo_mask()
    accs, mss = (acc0, acc1), (m0s, m1s)

    def k_of(kb, h):
        z = jnp.zeros_like(kb)
        return jnp.where(lo, kb, z) if h == 0 else jnp.where(lo, z, kb)

    def v_of(vb, h):
        ones = jnp.ones_like(vb)
        return jnp.where(lo, vb, ones) if h == 0 else jnp.where(lo, ones, vb)

    def softmax_block(q, kb, vb, bias, h):
        s = lax.dot_general(q, k_of(kb, h), NT_DIMS, preferred_element_type=F32) + bias
        m = jnp.max(s, axis=-1, keepdims=True)
        p = jnp.exp2(s - m)
        return m, jnp.dot(p.astype(BF16), v_of(vb, h), preferred_element_type=F32)

    def dense(qstart, kstart, klen, bias_ref):
        q = q_ref[pl.ds(qstart, tq), :]
        kb = k_ref[pl.ds(kstart, klen), :]
        vb = v_ref[pl.ds(kstart, klen), :]
        for h in range(2):
            m, a = softmax_block(q, kb, vb, bias_ref[...], h)
            accs[h][pl.ds(qstart, tq), :] = a
            mss[h][pl.ds(qstart, tq), :] = jnp.broadcast_to(m, (tq, LANES))

    dense(0, 0, tq, bdiag_ref)

    def body(qi, c):
        qs = pl.multiple_of(qi * tq, tq)
        dense(qs, qs - tq, 2 * tq, bfull_ref)
        return c

    lax.fori_loop(1, s_len // tq, body, 0)

    qf[...] = q_ref[...].astype(F32)
    kf[...] = k_ref[...].astype(F32)
    vf[...] = v_ref[...].astype(F32)
    for r in range(dil):
        rows = pl.ds(r, sub, stride=dil)
        q = qf[rows, :].astype(BF16)
        kb = kf[rows, :].astype(BF16)
        vb = vf[rows, :].astype(BF16)
        for h in range(2):
            mp, ap = softmax_block(q, kb, vb, bsub_ref[...], h)
            mo = mss[h][rows, :]
            mn = jnp.maximum(mo, mp)
            accs[h][rows, :] = jnp.exp2(mo - mn) * accs[h][rows, :] + jnp.exp2(mp - mn) * ap

    a0 = acc0[...]
    a1 = acc1[...]
    o = jnp.where(lo, a0 / pltpu.roll(a0, HALF, 1), a1 / pltpu.roll(a1, HALF, 1))
    o_ref[...] = o.astype(o_ref.dtype)


def _dilated_attention(pr, q_base, k_base, v_base, n_pairs, tq, out_width):
    b, s, _ = pr.shape
    dense_pat = tuple(p for p in DILATED_PATTERNS if p[0] <= tq)
    strided = tuple(p for p in DILATED_PATTERNS if p[0] > tq)
    assert len(strided) == 1 and s % strided[0][1] == 0
    dil = strided[0][1]
    sub = s // dil
    i = jnp.arange(tq, dtype=jnp.int32)[:, None]
    j = jnp.arange(2 * tq, dtype=jnp.int32)[None, :]
    bfull = _count_bias(i + tq - j, dense_pat)
    bdiag = bfull[:, tq:]
    si = jnp.arange(sub, dtype=jnp.int32)
    bsub = _count_bias((si[:, None] - si[None, :]) * dil, strided)
    seq = lambda base: pl.BlockSpec((None, s, LANES), lambda bi, pi, base=base: (bi, 0, base + pi))
    const = lambda a: pl.BlockSpec(a.shape, lambda bi, pi: (0, 0))
    return pl.pallas_call(
        functools.partial(_dilated_kernel, tq=tq, dil=dil, sub=sub),
        grid=(b, n_pairs),
        in_specs=[seq(q_base), seq(k_base), seq(v_base), const(bfull), const(bdiag), const(bsub)],
        out_specs=pl.BlockSpec((None, s, LANES), lambda bi, pi: (bi, 0, pi)),
        out_shape=jax.ShapeDtypeStruct((b, s, out_width), BF16),
        scratch_shapes=[pltpu.VMEM((s, LANES), F32)] * 7,
        compiler_params=pltpu.CompilerParams(
            dimension_semantics=("parallel", "parallel"), vmem_limit_bytes=VMEM_LIMIT),
    )(pr, pr, pr, bfull, bdiag, bsub)


def _hybrid_layer(x, layer_idx, tables, g, w_in, w_out, lq1, lk1, lq2, lk2, subln, tiles):
    b, s, d = x.shape
    x2 = x.reshape(b * s, d)
    pr = _hyb_proj(x2, g, w_in.astype(BF16), tables, tiles["tm"]).reshape(b, s, HYB_IN_WIDTH)
    cb = A_WIDTH // LANES
    out_a = _dilated_attention(pr, 0, cb, 2 * cb, A_HEADS // 2, tiles["tdil"], A_WIDTH)
    lam_init = 0.8 - 0.6 * math.exp(-0.3 * layer_idx)
    lam = jnp.exp(jnp.sum(lq1 * lk1)) - jnp.exp(jnp.sum(lq2 * lk2)) + lam_init
    out_b = _pair_attention([(pr, True, 3 * cb, 1), (pr, False, 4 * cb, 1), (pr, False, 5 * cb, 1)],
                            B_HEADS, split_qk=False, v_shared=True, tq=tiles["tq"], tk=tiles["tk"],
                            out_width=B_V_WIDTH, lam=jnp.full((1, LANES), lam, F32),
                            subln=subln.reshape(1, LANES), lam_init=lam_init)
    w_out = w_out.astype(BF16)
    y = _matmul_res([out_a.reshape(b * s, A_WIDTH), out_b.reshape(b * s, B_V_WIDTH)],
                    [w_out[:A_WIDTH], w_out[A_WIDTH:]], x2, tiles["tm"])
    return y.reshape(b, s, d)


def _mla_layer(x, tables, g, w_in, q_norm, w_uq, kv_norm, w_ukv, w_out, tiles):
    b, s, d = x.shape
    x2 = x.reshape(b * s, d)
    qf, kf, vf = _mla_proj(x2, g, w_in, q_norm, w_uq, kv_norm, w_ukv, tables, tiles["tm"])
    qf = qf.reshape(b, s, -1)
    kf = kf.reshape(b, s, -1)
    vf = vf.reshape(b, s, -1)
    o = _pair_attention([(qf, True, 0, 2), (qf, True, 1, 2), (kf, False, 0, 2), (kf, False, 1, 2),
                         (vf, False, 0, 1)], MLA_HEADS // 2, split_qk=True, v_shared=False,
                        tq=tiles["tq"], tk=tiles["tk"], out_width=MLA_HEADS * MLA_V_DIM)
    y = _matmul_res([o.reshape(b * s, MLA_HEADS * MLA_V_DIM)], [w_out.astype(BF16)], x2,
                    tiles["tm"])
    return y.reshape(b, s, d)


def _choose_tiles(s):
    tq = min(1024, s)
    return dict(tm=min(512, s), tq=tq, tk=min(512, tq), tdil=min(512, s))


def kernel(x, positions, attn_norm, ffn_norm, final_norm, hyb_w_in, hyb_w_out, diff_lambda_q1, diff_lambda_k1, diff_lambda_q2, diff_lambda_k2, diff_subln, mla_w_in, mla_q_norm, mla_w_uq, mla_kv_norm, mla_w_ukv, mla_w_out, ffn_w_up, ffn_conv_w, ffn_conv_b, ffn_w_down):
    depth = attn_norm.shape[0]
    tiles = _choose_tiles(x.shape[1])
    tab_p = _rope_tables(positions, PARTIAL_ROT_DIM, ROPE_THETA, HEAD_DIM, 0)
    tab_m = _rope_tables(positions, MLA_ROPE_DIM, MLA_ROPE_THETA, LANES, MLA_NOPE_DIM)
    for i in range(depth):
        j = i // 2
        if i % 2 == 0:
            x = _hybrid_layer(x, i, tab_p, attn_norm[i], hyb_w_in[j], hyb_w_out[j],
                              diff_lambda_q1[j], diff_lambda_k1[j], diff_lambda_q2[j],
                              diff_lambda_k2[j], diff_subln[j], tiles)
        else:
            x = _mla_layer(x, tab_m, attn_norm[i], mla_w_in[j], mla_q_norm[j], mla_w_uq[j],
                           mla_kv_norm[j], mla_w_ukv[j], mla_w_out[j], tiles)
        x = _ffn(x, ffn_norm[i], ffn_w_up[i], ffn_conv_w[i], ffn_conv_b[i], ffn_w_down[i],
                 final_norm, tiles["tm"], final_norm=(i == depth - 1))
    return x
```

```python
import functools
import math

import jax
import jax.numpy as jnp
from jax import lax
from jax.experimental import pallas as pl
from jax.experimental.pallas import tpu as pltpu

F32 = jnp.float32
BF16 = jnp.bfloat16

D_MODEL = 1024
HEAD_DIM = 64
A_HEADS = 8
A_WIDTH = A_HEADS * HEAD_DIM
DILATED_PATTERNS = ((128, 1), (512, 4), (2048, 16))
B_HEADS = 4
B_HEAD_DIM = 64
B_QK_WIDTH = B_HEADS * 2 * B_HEAD_DIM
B_V_WIDTH = B_HEADS * 2 * B_HEAD_DIM
HYB_IN_WIDTH = 3 * A_WIDTH + 2 * B_QK_WIDTH + B_V_WIDTH
ROPE_THETA = 500000.0
PARTIAL_ROT_DIM = HEAD_DIM // 4
MLA_HEADS = 16
MLA_Q_RANK = 256
MLA_KV_RANK = 128
MLA_NOPE_DIM = 64
MLA_ROPE_DIM = 32
MLA_V_DIM = 64
MLA_ROPE_THETA = 10000.0
MLA_IN_WIDTH = MLA_Q_RANK + MLA_KV_RANK + MLA_ROPE_DIM
FFN_DIM = 2816
CONV_WIDTH = 3
NORM_EPS = 1e-6
LOG2E = 1.4426950408889634

LANES = 128
HALF = LANES // 2
SUBLANES = 8
FFN_TILE = 256
VMEM_LIMIT = 56 * 1024 * 1024
NT_DIMS = (((1,), (1,)), ((), ()))


def _rms(x, g):
    ms = jnp.mean(x * x, axis=-1, keepdims=True)
    return (x * lax.rsqrt(ms + NORM_EPS)) * g


def _lo_mask():
    return lax.broadcasted_iota(jnp.int32, (1, LANES), 1) < HALF


def _rope_tables(positions, rot_dim, theta, period, lane0):
    half = rot_dim // 2
    inv_freq = theta ** (-jnp.arange(0, rot_dim, 2, dtype=F32) / rot_dim)
    ang = positions.astype(F32).reshape(-1, 1) * inv_freq
    cos, sin = jnp.cos(ang), jnp.sin(ang)
    t = ang.shape[0]
    one = lambda n: jnp.ones((t, n), F32)
    zero = lambda n: jnp.zeros((t, n), F32)
    tail = period - lane0 - rot_dim
    c = jnp.concatenate([one(lane0), cos, cos, one(tail)], axis=1)
    sa = jnp.concatenate([zero(lane0 + half), sin, zero(tail)], axis=1)
    sb = jnp.concatenate([zero(lane0), -sin, zero(half + tail)], axis=1)
    rep = LANES // period
    return tuple(jnp.tile(a, (1, rep)) for a in (c, sa, sb))


def _rope(x, c, sa, sb, half):
    return x * c + pltpu.roll(x, half, 1) * sa + pltpu.roll(x, LANES - half, 1) * sb


def _rope_tables_dup(positions, rot_dim, theta, lane0):
    inv_freq = theta ** (-jnp.arange(0, rot_dim, 2, dtype=F32) / rot_dim)
    ang = positions.astype(F32).reshape(-1, 1) * inv_freq
    cos, sin = jnp.cos(ang), jnp.sin(ang)
    t = ang.shape[0]
    tail = LANES - lane0 - rot_dim
    c = jnp.concatenate([jnp.ones((t, lane0), F32), cos, cos, jnp.zeros((t, tail), F32)], axis=1)
    sg = jnp.concatenate([jnp.zeros((t, lane0), F32), -sin, sin, jnp.zeros((t, tail), F32)], axis=1)
    return c, sg


def _rope_dup(x, c, sg, rot_dim):
    return x * c + pltpu.roll(x, LANES - rot_dim, 1) * sg


def _hyb_proj_kernel(x_ref, g_ref, w_ref, c_ref, sa_ref, sb_ref, o_ref, *, group, rot, qscale):
    h = _rms(x_ref[...], g_ref[...]).astype(BF16)
    c, sa, sb = c_ref[...], sa_ref[...], sb_ref[...]
    half = PARTIAL_ROT_DIM // 2
    for g0 in range(0, o_ref.shape[1], group):
        y = jnp.dot(h, w_ref[:, g0:g0 + group], preferred_element_type=F32)
        for off in range(0, group, LANES):
            blk = y[:, off:off + LANES]
            kind = rot[(g0 + off) // LANES]
            if kind:
                blk = _rope(blk, c, sa, sb, half)
            if kind == 2:
                blk = blk * qscale
            o_ref[:, g0 + off:g0 + off + LANES] = blk.astype(BF16)


def _hyb_proj(x2, g, w, tables, tm):
    m, d = x2.shape
    n = w.shape[1]
    cb = A_WIDTH // LANES
    rot = (2,) * cb + (1,) * cb + (0,) * cb + (2,) * cb + (1,) * cb + (0,) * cb
    row = lambda i: (i, 0)
    const = lambda i: (0, 0)
    return pl.pallas_call(
        functools.partial(_hyb_proj_kernel, group=4 * LANES, rot=rot,
                          qscale=HEAD_DIM ** -0.5 * LOG2E),
        grid=(m // tm,),
        in_specs=[pl.BlockSpec((tm, d), row), pl.BlockSpec((1, d), const),
                  pl.BlockSpec((d, n), const)] + [pl.BlockSpec((tm, LANES), row)] * 3,
        out_specs=pl.BlockSpec((tm, n), row),
        out_shape=jax.ShapeDtypeStruct((m, n), BF16),
        compiler_params=pltpu.CompilerParams(
            dimension_semantics=("parallel",), vmem_limit_bytes=VMEM_LIMIT),
    )(x2, g.reshape(1, d), w, *tables)


def _mla_proj_kernel(x_ref, g_ref, w_in_ref, qn_ref, w_uq_ref, kvn_ref, w_kv_ref,
                     c_ref, sg_ref, q_out, k_out, v_out, *, qscale):
    h = _rms(x_ref[...], g_ref[...]).astype(BF16)
    p = jnp.dot(h, w_in_ref[...], preferred_element_type=F32)
    c, sg = c_ref[...], sg_ref[...]
    cq = _rms(p[:, :MLA_Q_RANK], qn_ref[...]).astype(BF16)
    q = jnp.dot(cq, w_uq_ref[...], preferred_element_type=F32)
    for hd in range(MLA_HEADS):
        blk = q[:, hd * LANES:(hd + 1) * LANES]
        q_out[:, hd * LANES:(hd + 1) * LANES] = (
            _rope_dup(blk, c, sg, MLA_ROPE_DIM) * qscale).astype(BF16)
    ckv = _rms(p[:, MLA_Q_RANK:MLA_Q_RANK + MLA_KV_RANK], kvn_ref[...]).astype(BF16)
    kv = jnp.dot(ckv, w_kv_ref[...], preferred_element_type=F32)
    pe = _rope_dup(pltpu.roll(p[:, MLA_Q_RANK + MLA_KV_RANK:], MLA_NOPE_DIM, 1), c, sg,
                   MLA_ROPE_DIM)
    for hd in range(MLA_HEADS):
        k_out[:, hd * LANES:(hd + 1) * LANES] = (kv[:, hd * LANES:(hd + 1) * LANES] + pe).astype(BF16)
    v_out[...] = kv[:, MLA_HEADS * LANES:].astype(BF16)


def _mla_proj(x2, g, w_in, q_norm, w_uq, kv_norm, w_ukv, tables, tm):
    m, d = x2.shape
    hr = MLA_ROPE_DIM // 2
    swap = lambda w: jnp.concatenate([w[..., hr:], w[..., :hr]], axis=-1)
    w_pe = w_in[:, MLA_Q_RANK + MLA_KV_RANK:]
    in_pad = LANES - 2 * MLA_ROPE_DIM
    w_in_p = jnp.concatenate([w_in, swap(w_pe), jnp.zeros((d, in_pad), F32)], axis=1).astype(BF16)
    qd = MLA_NOPE_DIM + MLA_ROPE_DIM
    w_q3 = w_uq.reshape(MLA_Q_RANK, MLA_HEADS, qd)
    w_uq_p = jnp.concatenate([w_q3, swap(w_q3[..., MLA_NOPE_DIM:])], axis=-1)
    w_uq_p = w_uq_p.reshape(MLA_Q_RANK, MLA_HEADS * LANES)
    w_kv = w_ukv.reshape(MLA_KV_RANK, MLA_HEADS, MLA_NOPE_DIM + MLA_V_DIM)
    w_k = jnp.pad(w_kv[:, :, :MLA_NOPE_DIM], ((0, 0), (0, 0), (0, LANES - MLA_NOPE_DIM)))
    w_k = w_k.reshape(MLA_KV_RANK, MLA_HEADS * LANES)
    w_v = w_kv[:, :, MLA_NOPE_DIM:].reshape(MLA_KV_RANK, MLA_HEADS * MLA_V_DIM)
    w_kv_p = jnp.concatenate([w_k, w_v], axis=1).astype(BF16)
    qw = MLA_HEADS * LANES
    vw = MLA_HEADS * MLA_V_DIM
    row = lambda i: (i, 0)
    const = lambda i: (0, 0)
    full = lambda a: pl.BlockSpec(a.shape, const)
    args = (x2, g.reshape(1, d), w_in_p, q_norm.reshape(1, -1), w_uq_p.astype(BF16),
            kv_norm.reshape(1, -1), w_kv_p)
    return pl.pallas_call(
        functools.partial(_mla_proj_kernel, qscale=qd ** -0.5 * LOG2E),
        grid=(m // tm,),
        in_specs=[pl.BlockSpec((tm, d), row)] + [full(a) for a in args[1:]]
        + [pl.BlockSpec((tm, LANES), row)] * 2,
        out_specs=[pl.BlockSpec((tm, qw), row), pl.BlockSpec((tm, qw), row),
                   pl.BlockSpec((tm, vw), row)],
        out_shape=[jax.ShapeDtypeStruct((m, qw), BF16), jax.ShapeDtypeStruct((m, qw), BF16),
                   jax.ShapeDtypeStruct((m, vw), BF16)],
        compiler_params=pltpu.CompilerParams(
            dimension_semantics=("parallel",), vmem_limit_bytes=VMEM_LIMIT),
    )(*args, *tables)


def _ffn_kernel(*refs, n_mix, tm, tf, nf, ns, final_norm):
    a_refs = refs[:n_mix]
    wo_refs = refs[n_mix:2 * n_mix]
    (x_ref, g_ref, wu_ref, cw_ref, wd_ref, fg_ref, o_ref,
     h_scr, acc_scr, ubuf_a, ubuf_b, carry) = refs[2 * n_mix:]
    f = nf * tf
    s = pl.program_id(1)
    x = x_ref[...]
    for a_ref, wo_ref in zip(a_refs, wo_refs):
        x = x + jnp.dot(a_ref[...], wo_ref[...], preferred_element_type=F32)
    acc_scr[...] = x
    h_scr[...] = _rms(x, g_ref[...]).astype(BF16)

    @pl.when(s == 0)
    def _():
        carry[...] = jnp.zeros_like(carry)

    def up(j, ubuf):
        h = h_scr[...]
        for part in range(2):
            cols = pl.ds(pl.multiple_of(part * f + j * tf, LANES), tf)
            u = jnp.dot(h, wu_ref[:, cols], preferred_element_type=F32)
            for c in range(ns):
                idx = part * ns + c
                col = u[:, c * LANES:(c + 1) * LANES]
                ubuf[idx, 0:SUBLANES, :] = carry[j, idx]
                ubuf[idx, SUBLANES:SUBLANES + tm, :] = col
                carry[j, idx] = col[tm - SUBLANES:tm, :]

    def conv(ubuf, idx, cw):
        u0 = ubuf[idx, SUBLANES:SUBLANES + tm, :]
        u1 = ubuf[idx, SUBLANES - 1:SUBLANES - 1 + tm, :]
        u2 = ubuf[idx, SUBLANES - 2:SUBLANES - 2 + tm, :]
        return u2 * cw[0:1, :] + u1 * cw[1:2, :] + u0 * cw[2:3, :] + cw[3:4, :]

    def down(j, ubuf):
        cg = cw_ref[:, pl.ds(pl.multiple_of(j * tf, LANES), tf)]
        cv = cw_ref[:, pl.ds(pl.multiple_of(f + j * tf, LANES), tf)]
        parts = []
        for c in range(ns):
            gate = conv(ubuf, c, cg[:, c * LANES:(c + 1) * LANES])
            val = conv(ubuf, ns + c, cv[:, c * LANES:(c + 1) * LANES])
            parts.append(((gate / (1.0 + jnp.exp(-gate))) * val).astype(BF16))
        act = jnp.concatenate(parts, axis=1)
        rows = pl.ds(pl.multiple_of(j * tf, tf), tf)
        acc_scr[...] += jnp.dot(act, wd_ref[rows, :], preferred_element_type=F32)

    assert nf % 2 == 1
    up(0, ubuf_a)

    def body(jj, c):
        j = 2 * jj
        up(j + 1, ubuf_b)
        down(j, ubuf_a)
        up(j + 2, ubuf_a)
        down(j + 1, ubuf_b)
        return c

    lax.fori_loop(0, nf // 2, body, 0)
    down(nf - 1, ubuf_a)
    y = acc_scr[...]
    if final_norm:
        y = _rms(y, fg_ref[...])
    o_ref[...] = y


def _mix_ffn(mix, w_out, x, g, w_up, conv_w, conv_b, w_down, final_g, tm, final_norm):
    b, s, d = x.shape
    f = w_down.shape[0]
    tf = FFN_TILE
    nf = f // tf
    ns = tf // LANES
    w_out = w_out.astype(BF16)
    offs = [0]
    for a in mix:
        offs.append(offs[-1] + a.shape[-1])
    wos = [w_out[offs[i]:offs[i + 1]] for i in range(len(mix))]
    pad = jnp.zeros((SUBLANES - CONV_WIDTH - 1, 2 * f), F32)
    cw = jnp.concatenate([conv_w, conv_b[None, :], pad], axis=0)
    const2 = lambda bi, si: (0, 0)
    tile = lambda w: pl.BlockSpec((None, tm, w), lambda bi, si: (bi, si, 0))
    resident = lambda a: pl.BlockSpec(a.shape, const2, pipeline_mode=pl.Buffered(1))
    w_up = w_up.astype(BF16)
    w_down = w_down.astype(BF16)
    return pl.pallas_call(
        functools.partial(_ffn_kernel, n_mix=len(mix), tm=tm, tf=tf, nf=nf, ns=ns,
                          final_norm=final_norm),
        grid=(b, s // tm),
        in_specs=[tile(a.shape[-1]) for a in mix] + [resident(w) for w in wos] + [
            tile(d),
            pl.BlockSpec((1, d), const2),
            resident(w_up),
            pl.BlockSpec(cw.shape, const2),
            resident(w_down),
            pl.BlockSpec((1, d), const2),
        ],
        out_specs=tile(d),
        out_shape=jax.ShapeDtypeStruct((b, s, d), F32),
        scratch_shapes=[
            pltpu.VMEM((tm, d), BF16),
            pltpu.VMEM((tm, d), F32),
            pltpu.VMEM((2 * ns, tm + SUBLANES, LANES), F32),
            pltpu.VMEM((2 * ns, tm + SUBLANES, LANES), F32),
            pltpu.VMEM((nf, 2 * ns, SUBLANES, LANES), F32),
        ],
        compiler_params=pltpu.CompilerParams(
            dimension_semantics=("parallel", "arbitrary"), vmem_limit_bytes=VMEM_LIMIT),
    )(*mix, *wos, x, g.reshape(1, d), w_up, cw, w_down, final_g.reshape(1, d))


def _attn_kernel(*refs, split_qk, v_shared, pipelined, tq, tk, lam_init):
    if split_qk:
        q0_ref, q1_ref, k0_ref, k1_ref, v_ref = refs[:5]
        rest = refs[5:]
    else:
        q0_ref, k0_ref, v_ref = refs[:3]
        q1_ref, k1_ref = q0_ref, k0_ref
        rest = refs[3:]
    if v_shared:
        lam_ref, subln_ref, o_ref, acc0, acc1, mx0, mx1 = rest[:7]
    else:
        o_ref, acc0, acc1, mx0, mx1 = rest[:5]
    if pipelined:
        sa0, sa1, sb0, sb1 = rest[-4:]
        sbuf_a, sbuf_b = (sa0, sa1), (sb0, sb1)
    q_refs, k_refs, accs, mxs = (q0_ref, q1_ref), (k0_ref, k1_ref), (acc0, acc1), (mx0, mx1)
    acc_rep = acc0.shape[1] // LANES
    qi = pl.program_id(2)
    lo = _lo_mask()
    nkb = tq // tk

    def k_of(h, start, kt):
        kb = k_refs[h][pl.ds(start, kt), :]
        if split_qk:
            return kb
        z = jnp.zeros_like(kb)
        return jnp.where(lo, kb, z) if h == 0 else jnp.where(lo, z, kb)

    def v_of(h, start, kt):
        vb = v_ref[pl.ds(start, kt), :]
        ones = jnp.ones_like(vb)
        if v_shared:
            return jnp.concatenate([vb, ones], axis=1)
        return jnp.where(lo, vb, ones) if h == 0 else jnp.where(lo, ones, vb)

    def scores(h, kj, kt, r0):
        start = pl.multiple_of(kj * kt, kt)
        return lax.dot_general(q_refs[h][r0:tq, :], k_of(h, start, kt), NT_DIMS,
                               preferred_element_type=F32)

    def update(h, kj, kt, s, r0, masked):
        start = pl.multiple_of(kj * kt, kt)
        rows = tq - r0
        if masked:
            row = lax.broadcasted_iota(jnp.int32, (rows, kt), 0)
            col = lax.broadcasted_iota(jnp.int32, (rows, kt), 1)
            s = jnp.where(row >= col, s, -jnp.inf)
        m_old = mxs[h][r0:tq, :]
        m_new = jnp.maximum(m_old, jnp.max(s, axis=-1, keepdims=True))
        alpha = jnp.exp2(m_old - m_new)
        p = jnp.exp2(s - jnp.concatenate([m_new] * (kt // LANES), axis=1))
        if acc_rep > 1:
            alpha = jnp.concatenate([alpha] * acc_rep, axis=1)
        accs[h][r0:tq, :] = accs[h][r0:tq, :] * alpha + jnp.dot(
            p.astype(BF16), v_of(h, start, kt), preferred_element_type=F32)
        mxs[h][r0:tq, :] = m_new

    acc0[...] = jnp.zeros_like(acc0)
    acc1[...] = jnp.zeros_like(acc1)
    mx0[...] = jnp.full(mx0.shape, -jnp.inf, F32)
    mx1[...] = jnp.full(mx1.shape, -jnp.inf, F32)
    n = qi * nkb
    if pipelined:
        assert nkb % 2 == 0
        for h in range(2):
            sbuf_a[h][...] = scores(h, 0, tk, 0)

        def body(i, c):
            j = 2 * i
            for h in range(2):
                sbuf_b[h][...] = scores(h, j + 1, tk, 0)
            for h in range(2):
                update(h, j, tk, sbuf_a[h][...], 0, False)
            for h in range(2):
                sbuf_a[h][...] = scores(h, j + 2, tk, 0)
            for h in range(2):
                update(h, j + 1, tk, sbuf_b[h][...], 0, False)
            return c

        lax.fori_loop(0, n // 2, body, 0)
        for h in range(2):
            update(h, n, tk, sbuf_a[h][...], 0, True)
    else:
        def body(kj, c):
            for h in range(2):
                update(h, kj, tq, scores(h, kj, tq, 0), 0, False)
            return c

        lax.fori_loop(0, qi, body, 0)
        for h in range(2):
            update(h, n, tk, scores(h, n, tk, 0), 0, True)
    for d in range(1, nkb):
        for h in range(2):
            update(h, n + d, tk, scores(h, n + d, tk, d * tk), d * tk, True)
    a0 = acc0[...]
    a1 = acc1[...]
    if v_shared:
        o = a0[:, :LANES] / a0[:, LANES:] - lam_ref[...] * (a1[:, :LANES] / a1[:, LANES:])
        o = _rms(o, subln_ref[...]) * (1.0 - lam_init)
    else:
        o = jnp.where(lo, a0 / pltpu.roll(a0, HALF, 1), a1 / pltpu.roll(a1, HALF, 1))
    o_ref[...] = o.astype(o_ref.dtype)


def _pair_attention(ins, n_pairs, *, split_qk, v_shared, pipelined, tq, tk, out_width,
                    lam=None, subln=None, lam_init=0.0):
    b, s, _ = ins[0][0].shape
    in_specs = []
    for _, is_query, base, step in ins:
        if is_query:
            in_specs.append(pl.BlockSpec(
                (None, tq, LANES),
                lambda bi, pi, qi, base=base, step=step: (bi, qi, base + step * pi)))
        else:
            in_specs.append(pl.BlockSpec(
                (None, s, LANES),
                lambda bi, pi, qi, base=base, step=step: (bi, 0, base + step * pi)))
    args = [a for a, _, _, _ in ins]
    if v_shared:
        in_specs += [pl.BlockSpec((1, LANES), lambda bi, pi, qi: (0, 0))] * 2
        args += [lam, subln]
    acc_w = 2 * LANES if v_shared else LANES
    return pl.pallas_call(
        functools.partial(_attn_kernel, split_qk=split_qk, v_shared=v_shared,
                          pipelined=pipelined, tq=tq, tk=tk,
                          lam_init=lam_init),
        grid=(b, n_pairs, s // tq),
        in_specs=in_specs,
        out_specs=pl.BlockSpec((None, tq, LANES), lambda bi, pi, qi: (bi, qi, pi)),
        out_shape=jax.ShapeDtypeStruct((b, s, out_width), BF16),
        scratch_shapes=[pltpu.VMEM((tq, acc_w), F32), pltpu.VMEM((tq, acc_w), F32),
                        pltpu.VMEM((tq, LANES), F32), pltpu.VMEM((tq, LANES), F32)]
        + ([pltpu.VMEM((tq, tk), F32)] * 4 if pipelined else []),
        compiler_params=pltpu.CompilerParams(
            dimension_semantics=("parallel", "parallel", "arbitrary"),
            vmem_limit_bytes=VMEM_LIMIT),
    )(*args)


def _count_bias(delta, patterns):
    cnt = jnp.zeros(delta.shape, F32)
    for window, dil in patterns:
        n_back = window // dil
        hit = (delta >= 0) & (delta <= n_back * dil) & (delta % dil == 0)
        cnt = cnt + hit.astype(F32)
    return jnp.where(cnt > 0, jnp.log2(jnp.maximum(cnt, 1.0)), -jnp.inf)


def _dilated_kernel(q_ref, k_ref, v_ref, bfull_ref, bdiag_ref, bsub_ref, o_ref,
                    acc2_0, acc2_1, m2_0, m2_1, qf, kf, vf, sa0, sa1, sb0, sb1, *, tq, dil, sub):
    s_len = q_ref.shape[0]
    nq = s_len // tq
    lo = _lo_mask()
    acc2, m2 = (acc2_0, acc2_1), (m2_0, m2_1)
    sbuf_a, sbuf_b = (sa0, sa1), (sb0, sb1)

    def k_of(kb, h):
        z = jnp.zeros_like(kb)
        return jnp.where(lo, kb, z) if h == 0 else jnp.where(lo, z, kb)

    def v_of(vb, h):
        ones = jnp.ones_like(vb)
        return jnp.where(lo, vb, ones) if h == 0 else jnp.where(lo, ones, vb)

    def softmax_pv(s, vb, h):
        m = jnp.max(s, axis=-1, keepdims=True)
        p = jnp.exp2(s - m)
        return m, jnp.dot(p.astype(BF16), v_of(vb, h), preferred_element_type=F32)

    qf[...] = q_ref[...].astype(F32)
    kf[...] = k_ref[...].astype(F32)
    vf[...] = v_ref[...].astype(F32)
    for r in range(dil):
        rows = pl.ds(r, sub, stride=dil)
        q = qf[rows, :].astype(BF16)
        kb = kf[rows, :].astype(BF16)
        vb = vf[rows, :].astype(BF16)
        for h in range(2):
            s = lax.dot_general(q, k_of(kb, h), NT_DIMS, preferred_element_type=F32)
            mp, ap = softmax_pv(s + bsub_ref[...], vb, h)
            acc2[h][rows, :] = ap
            m2[h][rows, :] = jnp.broadcast_to(mp, (sub, LANES))

    def scores(qi, sbuf):
        qs = pl.multiple_of(qi * tq, tq)
        q = q_ref[pl.ds(qs, tq), :]
        kb = k_ref[pl.ds(qs - tq, 2 * tq), :]
        for h in range(2):
            sbuf[h][...] = lax.dot_general(q, k_of(kb, h), NT_DIMS, preferred_element_type=F32)

    def finish(qs, klo, klen, s_of, bias_ref):
        vb = v_ref[pl.ds(klo, klen), :]
        merged = []
        for h in range(2):
            m, a = softmax_pv(s_of(h) + bias_ref[...], vb, h)
            mo = m2[h][pl.ds(qs, tq), :]
            mn = jnp.maximum(mo, m)
            merged.append(jnp.exp2(m - mn) * a + jnp.exp2(mo - mn) * acc2[h][pl.ds(qs, tq), :])
        a0, a1 = merged
        o = jnp.where(lo, a0 / pltpu.roll(a0, HALF, 1), a1 / pltpu.roll(a1, HALF, 1))
        o_ref[pl.ds(qs, tq), :] = o.astype(o_ref.dtype)

    def finish_tile(qi, sbuf):
        qs = pl.multiple_of(qi * tq, tq)
        finish(qs, qs - tq, 2 * tq, lambda h: sbuf[h][...], bfull_ref)

    q_first = q_ref[0:tq, :]
    k_first = k_ref[0:tq, :]
    finish(0, 0, tq, lambda h: lax.dot_general(q_first, k_of(k_first, h), NT_DIMS,
                                               preferred_element_type=F32), bdiag_ref)
    assert nq % 2 == 0
    scores(1, sbuf_a)

    def body(i, c):
        b = 2 * i + 1
        scores(b + 1, sbuf_b)
        finish_tile(b, sbuf_a)
        scores(b + 2, sbuf_a)
        finish_tile(b + 1, sbuf_b)
        return c

    lax.fori_loop(0, (nq - 2) // 2, body, 0)
    finish_tile(nq - 1, sbuf_a)


def _dilated_attention(pr, q_base, k_base, v_base, n_pairs, tq, out_width):
    b, s, _ = pr.shape
    dense_pat = tuple(p for p in DILATED_PATTERNS if p[0] <= tq)
    strided = tuple(p for p in DILATED_PATTERNS if p[0] > tq)
    assert len(strided) == 1 and s % strided[0][1] == 0
    dil = strided[0][1]
    sub = s // dil
    i = jnp.arange(tq, dtype=jnp.int32)[:, None]
    j = jnp.arange(2 * tq, dtype=jnp.int32)[None, :]
    bfull = _count_bias(i + tq - j, dense_pat)
    bdiag = bfull[:, tq:]
    si = jnp.arange(sub, dtype=jnp.int32)
    bsub = _count_bias((si[:, None] - si[None, :]) * dil, strided)
    seq = lambda base: pl.BlockSpec((None, s, LANES), lambda bi, pi, base=base: (bi, 0, base + pi))
    const = lambda a: pl.BlockSpec(a.shape, lambda bi, pi: (0, 0))
    return pl.pallas_call(
        functools.partial(_dilated_kernel, tq=tq, dil=dil, sub=sub),
        grid=(b, n_pairs),
        in_specs=[seq(q_base), seq(k_base), seq(v_base), const(bfull), const(bdiag), const(bsub)],
        out_specs=pl.BlockSpec((None, s, LANES), lambda bi, pi: (bi, 0, pi)),
        out_shape=jax.ShapeDtypeStruct((b, s, out_width), BF16),
        scratch_shapes=[pltpu.VMEM((s, LANES), F32)] * 7 + [pltpu.VMEM((tq, 2 * tq), F32)] * 4,
        compiler_params=pltpu.CompilerParams(
            dimension_semantics=("parallel", "parallel"), vmem_limit_bytes=VMEM_LIMIT),
    )(pr, pr, pr, bfull, bdiag, bsub)


def _hybrid_mixer(x, layer_idx, tables, g, w_in, lq1, lk1, lq2, lk2, subln, tiles):
    b, s, d = x.shape
    x2 = x.reshape(b * s, d)
    pr = _hyb_proj(x2, g, w_in.astype(BF16), tables, tiles["tm"]).reshape(b, s, HYB_IN_WIDTH)
    cb = A_WIDTH // LANES
    out_a = _dilated_attention(pr, 0, cb, 2 * cb, A_HEADS // 2, tiles["tdil"], A_WIDTH)
    lam_init = 0.8 - 0.6 * math.exp(-0.3 * layer_idx)
    lam = jnp.exp(jnp.sum(lq1 * lk1)) - jnp.exp(jnp.sum(lq2 * lk2)) + lam_init
    out_b = _pair_attention([(pr, True, 3 * cb, 1), (pr, False, 4 * cb, 1), (pr, False, 5 * cb, 1)],
                            B_HEADS, split_qk=False, v_shared=True, pipelined=True,
                            tq=tiles["tq"], tk=tiles["tk"],
                            out_width=B_V_WIDTH, lam=jnp.full((1, LANES), lam, F32),
                            subln=subln.reshape(1, LANES), lam_init=lam_init)
    return [out_a, out_b]


def _mla_mixer(x, tables, g, w_in, q_norm, w_uq, kv_norm, w_ukv, tiles):
    b, s, d = x.shape
    x2 = x.reshape(b * s, d)
    qf, kf, vf = _mla_proj(x2, g, w_in, q_norm, w_uq, kv_norm, w_ukv, tables, tiles["tm"])
    qf = qf.reshape(b, s, -1)
    kf = kf.reshape(b, s, -1)
    vf = vf.reshape(b, s, -1)
    o = _pair_attention([(qf, True, 0, 2), (qf, True, 1, 2), (kf, False, 0, 2), (kf, False, 1, 2),
                         (vf, False, 0, 1)], MLA_HEADS // 2, split_qk=True, v_shared=False,
                        pipelined=False, tq=tiles["tq"], tk=tiles["tk"],
                        out_width=MLA_HEADS * MLA_V_DIM)
    return [o]


def _choose_tiles(s):
    tq = min(1024, s)
    return dict(tm=min(512, s), tf=min(1024, s), tq=tq, tk=min(512, tq), tdil=min(512, s))


def kernel(x, positions, attn_norm, ffn_norm, final_norm, hyb_w_in, hyb_w_out, diff_lambda_q1, diff_lambda_k1, diff_lambda_q2, diff_lambda_k2, diff_subln, mla_w_in, mla_q_norm, mla_w_uq, mla_kv_norm, mla_w_ukv, mla_w_out, ffn_w_up, ffn_conv_w, ffn_conv_b, ffn_w_down):
    depth = attn_norm.shape[0]
    tiles = _choose_tiles(x.shape[1])
    tab_p = _rope_tables(positions, PARTIAL_ROT_DIM, ROPE_THETA, HEAD_DIM, 0)
    tab_m = _rope_tables_dup(positions, MLA_ROPE_DIM, MLA_ROPE_THETA, MLA_NOPE_DIM)
    for i in range(depth):
        j = i // 2
        if i % 2 == 0:
            mix = _hybrid_mixer(x, i, tab_p, attn_norm[i], hyb_w_in[j], diff_lambda_q1[j],
                                diff_lambda_k1[j], diff_lambda_q2[j], diff_lambda_k2[j],
                                diff_subln[j], tiles)
            w_out = hyb_w_out[j]
        else:
            mix = _mla_mixer(x, tab_m, attn_norm[i], mla_w_in[j], mla_q_norm[j], mla_w_uq[j],
                             mla_kv_norm[j], mla_w_ukv[j], tiles)
            w_out = mla_w_out[j]
        x = _mix_ffn(mix, w_out, x, ffn_norm[i], ffn_w_up[i], ffn_conv_w[i], ffn_conv_b[i],
                     ffn_w_down[i], final_norm, tiles["tf"], final_norm=(i == depth - 1))
    return x
```

```python
import functools
import math

import jax
import jax.numpy as jnp
from jax import lax
from jax.experimental import pallas as pl
from jax.experimental.pallas import tpu as pltpu

F32 = jnp.float32
BF16 = jnp.bfloat16

D_MODEL = 1024
HEAD_DIM = 64
A_HEADS = 8
A_WIDTH = A_HEADS * HEAD_DIM
DILATED_PATTERNS = ((128, 1), (512, 4), (2048, 16))
B_HEADS = 4
B_HEAD_DIM = 64
B_QK_WIDTH = B_HEADS * 2 * B_HEAD_DIM
B_V_WIDTH = B_HEADS * 2 * B_HEAD_DIM
HYB_IN_WIDTH = 3 * A_WIDTH + 2 * B_QK_WIDTH + B_V_WIDTH
ROPE_THETA = 500000.0
PARTIAL_ROT_DIM = HEAD_DIM // 4
MLA_HEADS = 16
MLA_Q_RANK = 256
MLA_KV_RANK = 128
MLA_NOPE_DIM = 64
MLA_ROPE_DIM = 32
MLA_V_DIM = 64
MLA_ROPE_THETA = 10000.0
MLA_IN_WIDTH = MLA_Q_RANK + MLA_KV_RANK + MLA_ROPE_DIM
FFN_DIM = 2816
CONV_WIDTH = 3
NORM_EPS = 1e-6
LOG2E = 1.4426950408889634

LANES = 128
HALF = LANES // 2
SUBLANES = 8
FFN_TILE = 256
VMEM_LIMIT = 56 * 1024 * 1024
NT_DIMS = (((1,), (1,)), ((), ()))


def _rms(x, g):
    ms = jnp.mean(x * x, axis=-1, keepdims=True)
    return (x * lax.rsqrt(ms + NORM_EPS)) * g


def _lo_mask():
    return lax.broadcasted_iota(jnp.int32, (1, LANES), 1) < HALF


def _inv_freq(rot_dim, theta):
    return (theta ** (-jnp.arange(0, rot_dim, 2, dtype=F32) / rot_dim)).reshape(-1, 1)


def _lane_table(pieces, tm):
    rows = [jnp.full((p[0], tm), p[1], F32) if isinstance(p, tuple) else p
            for p in pieces if not (isinstance(p, tuple) and p[0] == 0)]
    return jnp.concatenate(rows, axis=0).T


def _rope_tables(pos, inv_freq, period, lane0):
    half = inv_freq.shape[0]
    tm = pos.shape[1]
    ang = inv_freq * pos
    cos, sin = jnp.cos(ang), jnp.sin(ang)
    tail = period - lane0 - 2 * half
    rep = LANES // period
    c = _lane_table([(lane0, 1.0), cos, cos, (tail, 1.0)] * rep, tm)
    sa = _lane_table([(lane0 + half, 0.0), sin, (tail, 0.0)] * rep, tm)
    sb = _lane_table([(lane0, 0.0), -sin, (half + tail, 0.0)] * rep, tm)
    return c, sa, sb


def _rope(x, c, sa, sb, half):
    return x * c + pltpu.roll(x, half, 1) * sa + pltpu.roll(x, LANES - half, 1) * sb


def _rope_tables_dup(pos, inv_freq, lane0):
    half = inv_freq.shape[0]
    tm = pos.shape[1]
    ang = inv_freq * pos
    cos, sin = jnp.cos(ang), jnp.sin(ang)
    tail = LANES - lane0 - 2 * half
    c = _lane_table([(lane0, 1.0), cos, cos, (tail, 0.0)], tm)
    sg = _lane_table([(lane0, 0.0), -sin, sin, (tail, 0.0)], tm)
    return c, sg


def _rope_dup(x, c, sg, rot_dim):
    return x * c + pltpu.roll(x, LANES - rot_dim, 1) * sg


def _hyb_proj_kernel(x_ref, g_ref, w_ref, pos_ref, invf_ref, o_ref, *, group, rot, qscale):
    h = _rms(x_ref[...], g_ref[...]).astype(BF16)
    c, sa, sb = _rope_tables(pos_ref[...], invf_ref[...], HEAD_DIM, 0)
    half = PARTIAL_ROT_DIM // 2
    for g0 in range(0, o_ref.shape[1], group):
        y = jnp.dot(h, w_ref[:, g0:g0 + group], preferred_element_type=F32)
        for off in range(0, group, LANES):
            blk = y[:, off:off + LANES]
            kind = rot[(g0 + off) // LANES]
            if kind:
                blk = _rope(blk, c, sa, sb, half)
            if kind == 2:
                blk = blk * qscale
            o_ref[:, g0 + off:g0 + off + LANES] = blk.astype(BF16)


def _hyb_proj(x2, g, w, pos, tm):
    m, d = x2.shape
    n = w.shape[1]
    cb = A_WIDTH // LANES
    rot = (2,) * cb + (1,) * cb + (0,) * cb + (2,) * cb + (1,) * cb + (0,) * cb
    row = lambda i: (i, 0)
    const = lambda i: (0, 0)
    return pl.pallas_call(
        functools.partial(_hyb_proj_kernel, group=4 * LANES, rot=rot,
                          qscale=HEAD_DIM ** -0.5 * LOG2E),
        grid=(m // tm,),
        in_specs=[pl.BlockSpec((tm, d), row), pl.BlockSpec((1, d), const),
                  pl.BlockSpec((d, n), const), pl.BlockSpec((1, tm), lambda i: (0, i)),
                  pl.BlockSpec((PARTIAL_ROT_DIM // 2, 1), const)],
        out_specs=pl.BlockSpec((tm, n), row),
        out_shape=jax.ShapeDtypeStruct((m, n), BF16),
        compiler_params=pltpu.CompilerParams(
            dimension_semantics=("parallel",), vmem_limit_bytes=VMEM_LIMIT),
    )(x2, g.reshape(1, d), w, pos, _inv_freq(PARTIAL_ROT_DIM, ROPE_THETA))


def _mla_proj_kernel(x_ref, g_ref, w_in_ref, qn_ref, w_uq_ref, kvn_ref, w_kv_ref,
                     pos_ref, invf_ref, q_out, k_out, v_out, *, qscale):
    h = _rms(x_ref[...], g_ref[...]).astype(BF16)
    p = jnp.dot(h, w_in_ref[...], preferred_element_type=F32)
    c, sg = _rope_tables_dup(pos_ref[...], invf_ref[...], MLA_NOPE_DIM)
    cq = _rms(p[:, :MLA_Q_RANK], qn_ref[...]).astype(BF16)
    q = jnp.dot(cq, w_uq_ref[...], preferred_element_type=F32)
    for hd in range(MLA_HEADS):
        blk = q[:, hd * LANES:(hd + 1) * LANES]
        q_out[:, hd * LANES:(hd + 1) * LANES] = (
            _rope_dup(blk, c, sg, MLA_ROPE_DIM) * qscale).astype(BF16)
    ckv = _rms(p[:, MLA_Q_RANK:MLA_Q_RANK + MLA_KV_RANK], kvn_ref[...]).astype(BF16)
    kv = jnp.dot(ckv, w_kv_ref[...], preferred_element_type=F32)
    pe = _rope_dup(pltpu.roll(p[:, MLA_Q_RANK + MLA_KV_RANK:], MLA_NOPE_DIM, 1), c, sg,
                   MLA_ROPE_DIM)
    for hd in range(MLA_HEADS):
        k_out[:, hd * LANES:(hd + 1) * LANES] = (kv[:, hd * LANES:(hd + 1) * LANES] + pe).astype(BF16)
    v_out[...] = kv[:, MLA_HEADS * LANES:].astype(BF16)


def _mla_proj(x2, g, w_in, q_norm, w_uq, kv_norm, w_ukv, pos, tm):
    m, d = x2.shape
    hr = MLA_ROPE_DIM // 2
    swap = lambda w: jnp.concatenate([w[..., hr:], w[..., :hr]], axis=-1)
    w_pe = w_in[:, MLA_Q_RANK + MLA_KV_RANK:]
    in_pad = LANES - 2 * MLA_ROPE_DIM
    w_in_p = jnp.concatenate([w_in, swap(w_pe), jnp.zeros((d, in_pad), F32)], axis=1).astype(BF16)
    qd = MLA_NOPE_DIM + MLA_ROPE_DIM
    w_q3 = w_uq.reshape(MLA_Q_RANK, MLA_HEADS, qd)
    w_uq_p = jnp.concatenate([w_q3, swap(w_q3[..., MLA_NOPE_DIM:])], axis=-1)
    w_uq_p = w_uq_p.reshape(MLA_Q_RANK, MLA_HEADS * LANES)
    w_kv = w_ukv.reshape(MLA_KV_RANK, MLA_HEADS, MLA_NOPE_DIM + MLA_V_DIM)
    w_k = jnp.pad(w_kv[:, :, :MLA_NOPE_DIM], ((0, 0), (0, 0), (0, LANES - MLA_NOPE_DIM)))
    w_k = w_k.reshape(MLA_KV_RANK, MLA_HEADS * LANES)
    w_v = w_kv[:, :, MLA_NOPE_DIM:].reshape(MLA_KV_RANK, MLA_HEADS * MLA_V_DIM)
    w_kv_p = jnp.concatenate([w_k, w_v], axis=1).astype(BF16)
    qw = MLA_HEADS * LANES
    vw = MLA_HEADS * MLA_V_DIM
    row = lambda i: (i, 0)
    const = lambda i: (0, 0)
    full = lambda a: pl.BlockSpec(a.shape, const)
    args = (x2, g.reshape(1, d), w_in_p, q_norm.reshape(1, -1), w_uq_p.astype(BF16),
            kv_norm.reshape(1, -1), w_kv_p)
    return pl.pallas_call(
        functools.partial(_mla_proj_kernel, qscale=qd ** -0.5 * LOG2E),
        grid=(m // tm,),
        in_specs=[pl.BlockSpec((tm, d), row)] + [full(a) for a in args[1:]]
        + [pl.BlockSpec((1, tm), lambda i: (0, i)), pl.BlockSpec((MLA_ROPE_DIM // 2, 1), const)],
        out_specs=[pl.BlockSpec((tm, qw), row), pl.BlockSpec((tm, qw), row),
                   pl.BlockSpec((tm, vw), row)],
        out_shape=[jax.ShapeDtypeStruct((m, qw), BF16), jax.ShapeDtypeStruct((m, qw), BF16),
                   jax.ShapeDtypeStruct((m, vw), BF16)],
        compiler_params=pltpu.CompilerParams(
            dimension_semantics=("parallel",), vmem_limit_bytes=VMEM_LIMIT),
    )(*args, pos, _inv_freq(MLA_ROPE_DIM, MLA_ROPE_THETA))


def _ffn_kernel(*refs, n_mix, tm, tf, nf, ns, final_norm):
    a_refs = refs[:n_mix]
    (wo_ref, x_ref, g_ref, wu_ref, cw_ref, wd_ref, fg_ref, o_ref,
     h_scr, acc_scr, ubuf_a, ubuf_b, carry) = refs[n_mix:]
    f = nf * tf
    s = pl.program_id(1)
    x = x_ref[...]
    row = 0
    for a_ref in a_refs:
        w = a_ref.shape[1]
        x = x + jnp.dot(a_ref[...], wo_ref[row:row + w, :], preferred_element_type=F32)
        row += w
    acc_scr[...] = x
    h_scr[...] = _rms(x, g_ref[...]).astype(BF16)

    @pl.when(s == 0)
    def _():
        carry[...] = jnp.zeros_like(carry)

    def up(j, ubuf):
        h = h_scr[...]
        for part in range(2):
            cols = pl.ds(pl.multiple_of(part * f + j * tf, LANES), tf)
            u = jnp.dot(h, wu_ref[:, cols], preferred_element_type=F32)
            for c in range(ns):
                idx = part * ns + c
                col = u[:, c * LANES:(c + 1) * LANES]
                ubuf[idx, 0:SUBLANES, :] = carry[j, idx]
                ubuf[idx, SUBLANES:SUBLANES + tm, :] = col
                carry[j, idx] = col[tm - SUBLANES:tm, :]

    def conv(ubuf, idx, cw):
        u0 = ubuf[idx, SUBLANES:SUBLANES + tm, :]
        u1 = ubuf[idx, SUBLANES - 1:SUBLANES - 1 + tm, :]
        u2 = ubuf[idx, SUBLANES - 2:SUBLANES - 2 + tm, :]
        return u2 * cw[0:1, :] + u1 * cw[1:2, :] + u0 * cw[2:3, :] + cw[3:4, :]

    def down(j, ubuf):
        cg = cw_ref[:, pl.ds(pl.multiple_of(j * tf, LANES), tf)]
        cv = cw_ref[:, pl.ds(pl.multiple_of(f + j * tf, LANES), tf)]
        parts = []
        for c in range(ns):
            gate = conv(ubuf, c, cg[:, c * LANES:(c + 1) * LANES])
            val = conv(ubuf, ns + c, cv[:, c * LANES:(c + 1) * LANES])
            parts.append(((gate / (1.0 + jnp.exp(-gate))) * val).astype(BF16))
        act = jnp.concatenate(parts, axis=1)
        rows = pl.ds(pl.multiple_of(j * tf, tf), tf)
        acc_scr[...] += jnp.dot(act, wd_ref[rows, :], preferred_element_type=F32)

    assert nf % 2 == 1
    up(0, ubuf_a)

    def body(jj, c):
        j = 2 * jj
        up(j + 1, ubuf_b)
        down(j, ubuf_a)
        up(j + 2, ubuf_a)
        down(j + 1, ubuf_b)
        return c

    lax.fori_loop(0, nf // 2, body, 0)
    down(nf - 1, ubuf_a)
    y = acc_scr[...]
    if final_norm:
        y = _rms(y, fg_ref[...])
    o_ref[...] = y


def _ffn_params(w_up, conv_w, conv_b, w_down):
    n_layers, _, f2 = conv_w.shape
    pad = jnp.zeros((n_layers, SUBLANES - CONV_WIDTH - 1, f2), F32)
    cw = jnp.concatenate([conv_w, conv_b[:, None, :], pad], axis=1)
    return w_up.astype(BF16), cw, w_down.astype(BF16)


def _mix_ffn(mix, w_out, x, g, ffn_params, layer, final_g, tm, final_norm):
    b, s, d = x.shape
    w_up, cw, w_down = ffn_params
    f = w_down.shape[1]
    tf = FFN_TILE
    nf = f // tf
    ns = tf // LANES
    w_out = w_out.astype(BF16)
    const2 = lambda bi, si: (0, 0)
    tile = lambda w: pl.BlockSpec((None, tm, w), lambda bi, si: (bi, si, 0))
    resident = lambda a: pl.BlockSpec(a.shape, const2, pipeline_mode=pl.Buffered(1))
    stacked = lambda a, **kw: pl.BlockSpec((None,) + a.shape[1:], lambda bi, si: (layer, 0, 0), **kw)
    return pl.pallas_call(
        functools.partial(_ffn_kernel, n_mix=len(mix), tm=tm, tf=tf, nf=nf, ns=ns,
                          final_norm=final_norm),
        grid=(b, s // tm),
        in_specs=[tile(a.shape[-1]) for a in mix] + [
            resident(w_out),
            tile(d),
            pl.BlockSpec((1, d), const2),
            stacked(w_up, pipeline_mode=pl.Buffered(1)),
            stacked(cw),
            stacked(w_down, pipeline_mode=pl.Buffered(1)),
            pl.BlockSpec((1, d), const2),
        ],
        out_specs=tile(d),
        out_shape=jax.ShapeDtypeStruct((b, s, d), F32),
        scratch_shapes=[
            pltpu.VMEM((tm, d), BF16),
            pltpu.VMEM((tm, d), F32),
            pltpu.VMEM((2 * ns, tm + SUBLANES, LANES), F32),
            pltpu.VMEM((2 * ns, tm + SUBLANES, LANES), F32),
            pltpu.VMEM((nf, 2 * ns, SUBLANES, LANES), F32),
        ],
        compiler_params=pltpu.CompilerParams(
            dimension_semantics=("parallel", "arbitrary"), vmem_limit_bytes=VMEM_LIMIT),
    )(*mix, w_out, x, g.reshape(1, d), w_up, cw, w_down, final_g.reshape(1, d))


def _attn_kernel(*refs, split_qk, v_shared, pipelined, tq, tk, lam_init):
    if split_qk:
        q0_ref, q1_ref, k0_ref, k1_ref, v_ref = refs[:5]
        rest = refs[5:]
    else:
        q0_ref, k0_ref, v_ref = refs[:3]
        q1_ref, k1_ref = q0_ref, k0_ref
        rest = refs[3:]
    if v_shared:
        lam_ref, subln_ref, o_ref, acc0, acc1, mx0, mx1 = rest[:7]
    else:
        o_ref, acc0, acc1, mx0, mx1 = rest[:5]
    if pipelined:
        sa0, sa1, sb0, sb1 = rest[-4:]
        sbuf_a, sbuf_b = (sa0, sa1), (sb0, sb1)
    q_refs, k_refs, accs, mxs = (q0_ref, q1_ref), (k0_ref, k1_ref), (acc0, acc1), (mx0, mx1)
    acc_rep = acc0.shape[1] // LANES
    qi = pl.program_id(2)
    lo = _lo_mask()
    nkb = tq // tk

    def k_of(h, start, kt):
        kb = k_refs[h][pl.ds(start, kt), :]
        if split_qk:
            return kb
        z = jnp.zeros_like(kb)
        return jnp.where(lo, kb, z) if h == 0 else jnp.where(lo, z, kb)

    def v_of(h, start, kt):
        vb = v_ref[pl.ds(start, kt), :]
        ones = jnp.ones_like(vb)
        if v_shared:
            return jnp.concatenate([vb, ones], axis=1)
        return jnp.where(lo, vb, ones) if h == 0 else jnp.where(lo, ones, vb)

    def scores(h, kj, kt, r0):
        start = pl.multiple_of(kj * kt, kt)
        return lax.dot_general(q_refs[h][r0:tq, :], k_of(h, start, kt), NT_DIMS,
                               preferred_element_type=F32)

    def update(h, kj, kt, s, r0, masked):
        start = pl.multiple_of(kj * kt, kt)
        rows = tq - r0
        if masked:
            row = lax.broadcasted_iota(jnp.int32, (rows, kt), 0)
            col = lax.broadcasted_iota(jnp.int32, (rows, kt), 1)
            s = jnp.where(row >= col, s, -jnp.inf)
        m_old = mxs[h][r0:tq, :]
        m_new = jnp.maximum(m_old, jnp.max(s, axis=-1, keepdims=True))
        alpha = jnp.exp2(m_old - m_new)
        p = jnp.exp2(s - jnp.concatenate([m_new] * (kt // LANES), axis=1))
        if acc_rep > 1:
            alpha = jnp.concatenate([alpha] * acc_rep, axis=1)
        accs[h][r0:tq, :] = accs[h][r0:tq, :] * alpha + jnp.dot(
            p.astype(BF16), v_of(h, start, kt), preferred_element_type=F32)
        mxs[h][r0:tq, :] = m_new

    acc0[...] = jnp.zeros_like(acc0)
    acc1[...] = jnp.zeros_like(acc1)
    mx0[...] = jnp.full(mx0.shape, -jnp.inf, F32)
    mx1[...] = jnp.full(mx1.shape, -jnp.inf, F32)
    n = qi * nkb
    if pipelined:
        assert nkb % 2 == 0
        for h in range(2):
            sbuf_a[h][...] = scores(h, 0, tk, 0)

        def body(i, c):
            j = 2 * i
            for h in range(2):
                sbuf_b[h][...] = scores(h, j + 1, tk, 0)
            for h in range(2):
                update(h, j, tk, sbuf_a[h][...], 0, False)
            for h in range(2):
                sbuf_a[h][...] = scores(h, j + 2, tk, 0)
            for h in range(2):
                update(h, j + 1, tk, sbuf_b[h][...], 0, False)
            return c

        lax.fori_loop(0, n // 2, body, 0)
        for h in range(2):
            update(h, n, tk, sbuf_a[h][...], 0, True)
    else:
        def body(kj, c):
            for h in range(2):
                update(h, kj, tq, scores(h, kj, tq, 0), 0, False)
            return c

        lax.fori_loop(0, qi, body, 0)
        for h in range(2):
            update(h, n, tk, scores(h, n, tk, 0), 0, True)
    for d in range(1, nkb):
        for h in range(2):
            update(h, n + d, tk, scores(h, n + d, tk, d * tk), d * tk, True)
    a0 = acc0[...]
    a1 = acc1[...]
    if v_shared:
        o = a0[:, :LANES] / a0[:, LANES:] - lam_ref[...] * (a1[:, :LANES] / a1[:, LANES:])
        o = _rms(o, subln_ref[...]) * (1.0 - lam_init)
    else:
        o = jnp.where(lo, a0 / pltpu.roll(a0, HALF, 1), a1 / pltpu.roll(a1, HALF, 1))
    o_ref[...] = o.astype(o_ref.dtype)


def _pair_attention(ins, n_pairs, *, split_qk, v_shared, pipelined, tq, tk, out_width,
                    lam=None, subln=None, lam_init=0.0):
    b, s, _ = ins[0][0].shape
    in_specs = []
    for _, is_query, base, step in ins:
        if is_query:
            in_specs.append(pl.BlockSpec(
                (None, tq, LANES),
                lambda bi, pi, qi, base=base, step=step: (bi, qi, base + step * pi)))
        else:
            in_specs.append(pl.BlockSpec(
                (None, s, LANES),
                lambda bi, pi, qi, base=base, step=step: (bi, 0, base + step * pi)))
    args = [a for a, _, _, _ in ins]
    if v_shared:
        in_specs += [pl.BlockSpec((1, LANES), lambda bi, pi, qi: (0, 0))] * 2
        args += [lam, subln]
    acc_w = 2 * LANES if v_shared else LANES
    return pl.pallas_call(
        functools.partial(_attn_kernel, split_qk=split_qk, v_shared=v_shared,
                          pipelined=pipelined, tq=tq, tk=tk,
                          lam_init=lam_init),
        grid=(b, n_pairs, s // tq),
        in_specs=in_specs,
        out_specs=pl.BlockSpec((None, tq, LANES), lambda bi, pi, qi: (bi, qi, pi)),
        out_shape=jax.ShapeDtypeStruct((b, s, out_width), BF16),
        scratch_shapes=[pltpu.VMEM((tq, acc_w), F32), pltpu.VMEM((tq, acc_w), F32),
                        pltpu.VMEM((tq, LANES), F32), pltpu.VMEM((tq, LANES), F32)]
        + ([pltpu.VMEM((tq, tk), F32)] * 4 if pipelined else []),
        compiler_params=pltpu.CompilerParams(
            dimension_semantics=("parallel", "parallel", "arbitrary"),
            vmem_limit_bytes=VMEM_LIMIT),
    )(*args)


def _count_bias(delta, patterns):
    cnt = jnp.zeros(delta.shape, F32)
    for window, dil in patterns:
        n_back = window // dil
        hit = (delta >= 0) & (delta <= n_back * dil) & (delta % dil == 0)
        cnt = cnt + hit.astype(F32)
    return jnp.where(cnt > 0, jnp.log2(jnp.maximum(cnt, 1.0)), -jnp.inf)


def _dilated_kernel(q_ref, k_ref, v_ref, bfull_ref, bdiag_ref, bsub_ref, o_ref,
                    acc2_0, acc2_1, m2_0, m2_1, qf, kf, vf, sa0, sa1, sb0, sb1, *, tq, dil, sub):
    s_len = q_ref.shape[0]
    nq = s_len // tq
    lo = _lo_mask()
    acc2, m2 = (acc2_0, acc2_1), (m2_0, m2_1)
    sbuf_a, sbuf_b = (sa0, sa1), (sb0, sb1)

    def k_of(kb, h):
        z = jnp.zeros_like(kb)
        return jnp.where(lo, kb, z) if h == 0 else jnp.where(lo, z, kb)

    def v_of(vb, h):
        ones = jnp.ones_like(vb)
        return jnp.where(lo, vb, ones) if h == 0 else jnp.where(lo, ones, vb)

    def softmax_pv(s, vb, h):
        m = jnp.max(s, axis=-1, keepdims=True)
        p = jnp.exp2(s - m)
        return m, jnp.dot(p.astype(BF16), v_of(vb, h), preferred_element_type=F32)

    qf[...] = q_ref[...].astype(F32)
    kf[...] = k_ref[...].astype(F32)
    vf[...] = v_ref[...].astype(F32)
    for r in range(dil):
        rows = pl.ds(r, sub, stride=dil)
        q = qf[rows, :].astype(BF16)
        kb = kf[rows, :].astype(BF16)
        vb = vf[rows, :].astype(BF16)
        for h in range(2):
            s = lax.dot_general(q, k_of(kb, h), NT_DIMS, preferred_element_type=F32)
            mp, ap = softmax_pv(s + bsub_ref[...], vb, h)
            acc2[h][rows, :] = ap
            m2[h][rows, :] = jnp.broadcast_to(mp, (sub, LANES))

    def scores(qi, sbuf):
        qs = pl.multiple_of(qi * tq, tq)
        q = q_ref[pl.ds(qs, tq), :]
        kb = k_ref[pl.ds(qs - tq, 2 * tq), :]
        for h in range(2):
            sbuf[h][...] = lax.dot_general(q, k_of(kb, h), NT_DIMS, preferred_element_type=F32)

    def finish(qs, klo, klen, s_of, bias_ref):
        vb = v_ref[pl.ds(klo, klen), :]
        merged = []
        for h in range(2):
            m, a = softmax_pv(s_of(h) + bias_ref[...], vb, h)
            mo = m2[h][pl.ds(qs, tq), :]
            mn = jnp.maximum(mo, m)
            merged.append(jnp.exp2(m - mn) * a + jnp.exp2(mo - mn) * acc2[h][pl.ds(qs, tq), :])
        a0, a1 = merged
        o = jnp.where(lo, a0 / pltpu.roll(a0, HALF, 1), a1 / pltpu.roll(a1, HALF, 1))
        o_ref[pl.ds(qs, tq), :] = o.astype(o_ref.dtype)

    def finish_tile(qi, sbuf):
        qs = pl.multiple_of(qi * tq, tq)
        finish(qs, qs - tq, 2 * tq, lambda h: sbuf[h][...], bfull_ref)

    q_first = q_ref[0:tq, :]
    k_first = k_ref[0:tq, :]
    finish(0, 0, tq, lambda h: lax.dot_general(q_first, k_of(k_first, h), NT_DIMS,
                                               preferred_element_type=F32), bdiag_ref)
    assert nq % 2 == 0
    scores(1, sbuf_a)

    def body(i, c):
        b = 2 * i + 1
        scores(b + 1, sbuf_b)
        finish_tile(b, sbuf_a)
        scores(b + 2, sbuf_a)
        finish_tile(b + 1, sbuf_b)
        return c

    lax.fori_loop(0, (nq - 2) // 2, body, 0)
    finish_tile(nq - 1, sbuf_a)


def _dilated_attention(pr, q_base, k_base, v_base, n_pairs, tq, out_width):
    b, s, _ = pr.shape
    dense_pat = tuple(p for p in DILATED_PATTERNS if p[0] <= tq)
    strided = tuple(p for p in DILATED_PATTERNS if p[0] > tq)
    assert len(strided) == 1 and s % strided[0][1] == 0
    dil = strided[0][1]
    sub = s // dil
    i = jnp.arange(tq, dtype=jnp.int32)[:, None]
    j = jnp.arange(2 * tq, dtype=jnp.int32)[None, :]
    bfull = _count_bias(i + tq - j, dense_pat)
    bdiag = bfull[:, tq:]
    si = jnp.arange(sub, dtype=jnp.int32)
    bsub = _count_bias((si[:, None] - si[None, :]) * dil, strided)
    seq = lambda base: pl.BlockSpec((None, s, LANES), lambda bi, pi, base=base: (bi, 0, base + pi))
    const = lambda a: pl.BlockSpec(a.shape, lambda bi, pi: (0, 0))
    return pl.pallas_call(
        functools.partial(_dilated_kernel, tq=tq, dil=dil, sub=sub),
        grid=(b, n_pairs),
        in_specs=[seq(q_base), seq(k_base), seq(v_base), const(bfull), const(bdiag), const(bsub)],
        out_specs=pl.BlockSpec((None, s, LANES), lambda bi, pi: (bi, 0, pi)),
        out_shape=jax.ShapeDtypeStruct((b, s, out_width), BF16),
        scratch_shapes=[pltpu.VMEM((s, LANES), F32)] * 7 + [pltpu.VMEM((tq, 2 * tq), F32)] * 4,
        compiler_params=pltpu.CompilerParams(
            dimension_semantics=("parallel", "parallel"), vmem_limit_bytes=VMEM_LIMIT),
    )(pr, pr, pr, bfull, bdiag, bsub)


def _hybrid_mixer(x, layer_idx, pos, g, w_in, lq1, lk1, lq2, lk2, subln, tiles):
    b, s, d = x.shape
    x2 = x.reshape(b * s, d)
    pr = _hyb_proj(x2, g, w_in.astype(BF16), pos, tiles["tm"]).reshape(b, s, HYB_IN_WIDTH)
    cb = A_WIDTH // LANES
    out_a = _dilated_attention(pr, 0, cb, 2 * cb, A_HEADS // 2, tiles["tdil"], A_WIDTH)
    lam_init = 0.8 - 0.6 * math.exp(-0.3 * layer_idx)
    lam = jnp.exp(jnp.sum(lq1 * lk1)) - jnp.exp(jnp.sum(lq2 * lk2)) + lam_init
    out_b = _pair_attention([(pr, True, 3 * cb, 1), (pr, False, 4 * cb, 1), (pr, False, 5 * cb, 1)],
                            B_HEADS, split_qk=False, v_shared=True, pipelined=True,
                            tq=tiles["tq"], tk=tiles["tk"],
                            out_width=B_V_WIDTH, lam=jnp.full((1, LANES), lam, F32),
                            subln=subln.reshape(1, LANES), lam_init=lam_init)
    return [out_a, out_b]


def _mla_mixer(x, pos, g, w_in, q_norm, w_uq, kv_norm, w_ukv, tiles):
    b, s, d = x.shape
    x2 = x.reshape(b * s, d)
    qf, kf, vf = _mla_proj(x2, g, w_in, q_norm, w_uq, kv_norm, w_ukv, pos, tiles["tm"])
    qf = qf.reshape(b, s, -1)
    kf = kf.reshape(b, s, -1)
    vf = vf.reshape(b, s, -1)
    o = _pair_attention([(qf, True, 0, 2), (qf, True, 1, 2), (kf, False, 0, 2), (kf, False, 1, 2),
                         (vf, False, 0, 1)], MLA_HEADS // 2, split_qk=True, v_shared=False,
                        pipelined=False, tq=tiles["tq"], tk=tiles["tk"],
                        out_width=MLA_HEADS * MLA_V_DIM)
    return [o]


def _choose_tiles(s):
    tq = min(1024, s)
    return dict(tm=min(1024, s), tf=min(1024, s), tq=tq, tk=min(512, tq), tdil=min(512, s))


def kernel(x, positions, attn_norm, ffn_norm, final_norm, hyb_w_in, hyb_w_out, diff_lambda_q1, diff_lambda_k1, diff_lambda_q2, diff_lambda_k2, diff_subln, mla_w_in, mla_q_norm, mla_w_uq, mla_kv_norm, mla_w_ukv, mla_w_out, ffn_w_up, ffn_conv_w, ffn_conv_b, ffn_w_down):
    depth = attn_norm.shape[0]
    tiles = _choose_tiles(x.shape[1])
    pos = positions.astype(F32).reshape(1, -1)
    ffn_params = _ffn_params(ffn_w_up, ffn_conv_w, ffn_conv_b, ffn_w_down)
    for i in range(depth):
        j = i // 2
        if i % 2 == 0:
            mix = _hybrid_mixer(x, i, pos, attn_norm[i], hyb_w_in[j], diff_lambda_q1[j],
                                diff_lambda_k1[j], diff_lambda_q2[j], diff_lambda_k2[j],
                                diff_subln[j], tiles)
            w_out = hyb_w_out[j]
        else:
            mix = _mla_mixer(x, pos, attn_norm[i], mla_w_in[j], mla_q_norm[j], mla_w_uq[j],
                             mla_kv_norm[j], mla_w_ukv[j], tiles)
            w_out = mla_w_out[j]
        x = _mix_ffn(mix, w_out, x, ffn_norm[i], ffn_params, i, final_norm, tiles["tf"],
                     final_norm=(i == depth - 1))
    return x
```

```python
import functools
import math

import jax
import jax.numpy as jnp
from jax import lax
from jax.experimental import pallas as pl
from jax.experimental.pallas import tpu as pltpu

F32 = jnp.float32
BF16 = jnp.bfloat16

D_MODEL = 1024
HEAD_DIM = 64
A_HEADS = 8
A_WIDTH = A_HEADS * HEAD_DIM
DILATED_PATTERNS = ((128, 1), (512, 4), (2048, 16))
B_HEADS = 4
B_HEAD_DIM = 64
B_QK_WIDTH = B_HEADS * 2 * B_HEAD_DIM
B_V_WIDTH = B_HEADS * 2 * B_HEAD_DIM
HYB_IN_WIDTH = 3 * A_WIDTH + 2 * B_QK_WIDTH + B_V_WIDTH
ROPE_THETA = 500000.0
PARTIAL_ROT_DIM = HEAD_DIM // 4
MLA_HEADS = 16
MLA_Q_RANK = 256
MLA_KV_RANK = 128
MLA_NOPE_DIM = 64
MLA_ROPE_DIM = 32
MLA_V_DIM = 64
MLA_ROPE_THETA = 10000.0
MLA_IN_WIDTH = MLA_Q_RANK + MLA_KV_RANK + MLA_ROPE_DIM
FFN_DIM = 2816
CONV_WIDTH = 3
NORM_EPS = 1e-6
LOG2E = 1.4426950408889634

LANES = 128
HALF = LANES // 2
SUBLANES = 8
FFN_TILE = 256
VMEM_LIMIT = 56 * 1024 * 1024
NT_DIMS = (((1,), (1,)), ((), ()))


def _rms(x, g):
    ms = jnp.mean(x * x, axis=-1, keepdims=True)
    return (x * lax.rsqrt(ms + NORM_EPS)) * g


def _lo_mask():
    return lax.broadcasted_iota(jnp.int32, (1, LANES), 1) < HALF


def _inv_freq(rot_dim, theta):
    return (theta ** (-jnp.arange(0, rot_dim, 2, dtype=F32) / rot_dim)).reshape(-1, 1)


def _lane_table(pieces, tm):
    rows = [jnp.full((p[0], tm), p[1], F32) if isinstance(p, tuple) else p
            for p in pieces if not (isinstance(p, tuple) and p[0] == 0)]
    return jnp.concatenate(rows, axis=0).T


def _rope_tables(pos, inv_freq, period, lane0):
    half = inv_freq.shape[0]
    tm = pos.shape[1]
    ang = inv_freq * pos
    cos, sin = jnp.cos(ang), jnp.sin(ang)
    tail = period - lane0 - 2 * half
    rep = LANES // period
    c = _lane_table([(lane0, 1.0), cos, cos, (tail, 1.0)] * rep, tm)
    sa = _lane_table([(lane0 + half, 0.0), sin, (tail, 0.0)] * rep, tm)
    sb = _lane_table([(lane0, 0.0), -sin, (half + tail, 0.0)] * rep, tm)
    return c, sa, sb


def _rope(x, c, sa, sb, half):
    return x * c + pltpu.roll(x, half, 1) * sa + pltpu.roll(x, LANES - half, 1) * sb


def _rope_tables_dup(pos, inv_freq, lane0):
    half = inv_freq.shape[0]
    tm = pos.shape[1]
    ang = inv_freq * pos
    cos, sin = jnp.cos(ang), jnp.sin(ang)
    tail = LANES - lane0 - 2 * half
    c = _lane_table([(lane0, 1.0), cos, cos, (tail, 0.0)], tm)
    sg = _lane_table([(lane0, 0.0), -sin, sin, (tail, 0.0)], tm)
    return c, sg


def _rope_dup(x, c, sg, rot_dim):
    return x * c + pltpu.roll(x, LANES - rot_dim, 1) * sg


def _hyb_proj_kernel(x_ref, g_ref, w_ref, pos_ref, invf_ref, o_ref, *, group, rot, qscale):
    h = _rms(x_ref[...], g_ref[...]).astype(BF16)
    c, sa, sb = _rope_tables(pos_ref[...], invf_ref[...], HEAD_DIM, 0)
    half = PARTIAL_ROT_DIM // 2
    for g0 in range(0, o_ref.shape[1], group):
        y = jnp.dot(h, w_ref[:, g0:g0 + group], preferred_element_type=F32)
        for off in range(0, group, LANES):
            blk = y[:, off:off + LANES]
            kind = rot[(g0 + off) // LANES]
            if kind:
                blk = _rope(blk, c, sa, sb, half)
            if kind == 2:
                blk = blk * qscale
            o_ref[:, g0 + off:g0 + off + LANES] = blk.astype(BF16)


def _hyb_proj(x2, g, w, pos, tm):
    m, d = x2.shape
    n = w.shape[1]
    cb = A_WIDTH // LANES
    rot = (2,) * cb + (1,) * cb + (0,) * cb + (2,) * cb + (1,) * cb + (0,) * cb
    row = lambda i: (i, 0)
    const = lambda i: (0, 0)
    return pl.pallas_call(
        functools.partial(_hyb_proj_kernel, group=4 * LANES, rot=rot,
                          qscale=HEAD_DIM ** -0.5 * LOG2E),
        grid=(m // tm,),
        in_specs=[pl.BlockSpec((tm, d), row), pl.BlockSpec((1, d), const),
                  pl.BlockSpec((d, n), const), pl.BlockSpec((1, tm), lambda i: (0, i)),
                  pl.BlockSpec((PARTIAL_ROT_DIM // 2, 1), const)],
        out_specs=pl.BlockSpec((tm, n), row),
        out_shape=jax.ShapeDtypeStruct((m, n), BF16),
        compiler_params=pltpu.CompilerParams(
            dimension_semantics=("parallel",), vmem_limit_bytes=VMEM_LIMIT),
    )(x2, g.reshape(1, d), w, pos, _inv_freq(PARTIAL_ROT_DIM, ROPE_THETA))


def _mla_proj_kernel(x_ref, g_ref, w_in_ref, qn_ref, w_uq_ref, kvn_ref, w_kv_ref,
                     pos_ref, invf_ref, q_out, k_out, v_out, *, qscale):
    h = _rms(x_ref[...], g_ref[...]).astype(BF16)
    p = jnp.dot(h, w_in_ref[...], preferred_element_type=F32)
    c, sg = _rope_tables_dup(pos_ref[...], invf_ref[...], MLA_NOPE_DIM)
    cq = _rms(p[:, :MLA_Q_RANK], qn_ref[...]).astype(BF16)
    q = jnp.dot(cq, w_uq_ref[...], preferred_element_type=F32)
    for hd in range(MLA_HEADS):
        blk = q[:, hd * LANES:(hd + 1) * LANES]
        q_out[:, hd * LANES:(hd + 1) * LANES] = (
            _rope_dup(blk, c, sg, MLA_ROPE_DIM) * qscale).astype(BF16)
    ckv = _rms(p[:, MLA_Q_RANK:MLA_Q_RANK + MLA_KV_RANK], kvn_ref[...]).astype(BF16)
    kv = jnp.dot(ckv, w_kv_ref[...], preferred_element_type=F32)
    pe = _rope_dup(pltpu.roll(p[:, MLA_Q_RANK + MLA_KV_RANK:], MLA_NOPE_DIM, 1), c, sg,
                   MLA_ROPE_DIM)
    for hd in range(MLA_HEADS):
        k_out[:, hd * LANES:(hd + 1) * LANES] = (kv[:, hd * LANES:(hd + 1) * LANES] + pe).astype(BF16)
    v_out[...] = kv[:, MLA_HEADS * LANES:].astype(BF16)


def _mla_proj(x2, g, w_in, q_norm, w_uq, kv_norm, w_ukv, pos, tm):
    m, d = x2.shape
    hr = MLA_ROPE_DIM // 2
    swap = lambda w: jnp.concatenate([w[..., hr:], w[..., :hr]], axis=-1)
    w_pe = w_in[:, MLA_Q_RANK + MLA_KV_RANK:]
    in_pad = LANES - 2 * MLA_ROPE_DIM
    w_in_p = jnp.concatenate([w_in, swap(w_pe), jnp.zeros((d, in_pad), F32)], axis=1).astype(BF16)
    qd = MLA_NOPE_DIM + MLA_ROPE_DIM
    w_q3 = w_uq.reshape(MLA_Q_RANK, MLA_HEADS, qd)
    w_uq_p = jnp.concatenate([w_q3, swap(w_q3[..., MLA_NOPE_DIM:])], axis=-1)
    w_uq_p = w_uq_p.reshape(MLA_Q_RANK, MLA_HEADS * LANES)
    w_kv = w_ukv.reshape(MLA_KV_RANK, MLA_HEADS, MLA_NOPE_DIM + MLA_V_DIM)
    w_k = jnp.pad(w_kv[:, :, :MLA_NOPE_DIM], ((0, 0), (0, 0), (0, LANES - MLA_NOPE_DIM)))
    w_k = w_k.reshape(MLA_KV_RANK, MLA_HEADS * LANES)
    w_v = w_kv[:, :, MLA_NOPE_DIM:].reshape(MLA_KV_RANK, MLA_HEADS * MLA_V_DIM)
    w_kv_p = jnp.concatenate([w_k, w_v], axis=1).astype(BF16)
    qw = MLA_HEADS * LANES
    vw = MLA_HEADS * MLA_V_DIM
    row = lambda i: (i, 0)
    const = lambda i: (0, 0)
    full = lambda a: pl.BlockSpec(a.shape, const)
    args = (x2, g.reshape(1, d), w_in_p, q_norm.reshape(1, -1), w_uq_p.astype(BF16),
            kv_norm.reshape(1, -1), w_kv_p)
    return pl.pallas_call(
        functools.partial(_mla_proj_kernel, qscale=qd ** -0.5 * LOG2E),
        grid=(m // tm,),
        in_specs=[pl.BlockSpec((tm, d), row)] + [full(a) for a in args[1:]]
        + [pl.BlockSpec((1, tm), lambda i: (0, i)), pl.BlockSpec((MLA_ROPE_DIM // 2, 1), const)],
        out_specs=[pl.BlockSpec((tm, qw), row), pl.BlockSpec((tm, qw), row),
                   pl.BlockSpec((tm, vw), row)],
        out_shape=[jax.ShapeDtypeStruct((m, qw), BF16), jax.ShapeDtypeStruct((m, qw), BF16),
                   jax.ShapeDtypeStruct((m, vw), BF16)],
        compiler_params=pltpu.CompilerParams(
            dimension_semantics=("parallel",), vmem_limit_bytes=VMEM_LIMIT),
    )(*args, pos, _inv_freq(MLA_ROPE_DIM, MLA_ROPE_THETA))


def _ffn_kernel(*refs, n_mix, tm, tf, nf, ns, final_norm):
    a_refs = refs[:n_mix]
    (wo_ref, x_ref, g_ref, wu_ref, cw_ref, wd_ref, fg_ref, o_ref,
     h_scr, acc_scr, ubuf_a, ubuf_b, carry) = refs[n_mix:]
    f = nf * tf
    s = pl.program_id(1)
    x = x_ref[...]
    row = 0
    for a_ref in a_refs:
        w = a_ref.shape[1]
        x = x + jnp.dot(a_ref[...], wo_ref[row:row + w, :], preferred_element_type=F32)
        row += w
    acc_scr[...] = x
    h_scr[...] = _rms(x, g_ref[...]).astype(BF16)

    @pl.when(s == 0)
    def _():
        carry[...] = jnp.zeros_like(carry)

    def up(j, ubuf):
        h = h_scr[...]
        for part in range(2):
            cols = pl.ds(pl.multiple_of(part * f + j * tf, LANES), tf)
            u = jnp.dot(h, wu_ref[:, cols], preferred_element_type=F32)
            for c in range(ns):
                idx = part * ns + c
                col = u[:, c * LANES:(c + 1) * LANES]
                ubuf[idx, 0:SUBLANES, :] = carry[j, idx]
                ubuf[idx, SUBLANES:SUBLANES + tm, :] = col
                carry[j, idx] = col[tm - SUBLANES:tm, :]

    def conv(ubuf, idx, cw):
        u0 = ubuf[idx, SUBLANES:SUBLANES + tm, :]
        u1 = ubuf[idx, SUBLANES - 1:SUBLANES - 1 + tm, :]
        u2 = ubuf[idx, SUBLANES - 2:SUBLANES - 2 + tm, :]
        return u2 * cw[0:1, :] + u1 * cw[1:2, :] + u0 * cw[2:3, :] + cw[3:4, :]

    def down(j, ubuf):
        cg = cw_ref[:, pl.ds(pl.multiple_of(j * tf, LANES), tf)]
        cv = cw_ref[:, pl.ds(pl.multiple_of(f + j * tf, LANES), tf)]
        parts = []
        for c in range(ns):
            gate = conv(ubuf, c, cg[:, c * LANES:(c + 1) * LANES])
            val = conv(ubuf, ns + c, cv[:, c * LANES:(c + 1) * LANES])
            parts.append(((gate / (1.0 + jnp.exp(-gate))) * val).astype(BF16))
        act = jnp.concatenate(parts, axis=1)
        rows = pl.ds(pl.multiple_of(j * tf, tf), tf)
        acc_scr[...] += jnp.dot(act, wd_ref[rows, :], preferred_element_type=F32)

    assert nf % 2 == 1
    up(0, ubuf_a)

    def body(jj, c):
        j = 2 * jj
        up(j + 1, ubuf_b)
        down(j, ubuf_a)
        up(j + 2, ubuf_a)
        down(j + 1, ubuf_b)
        return c

    lax.fori_loop(0, nf // 2, body, 0)
    down(nf - 1, ubuf_a)
    y = acc_scr[...]
    if final_norm:
        y = _rms(y, fg_ref[...])
    o_ref[...] = y


def _ffn_params(w_up, conv_w, conv_b, w_down):
    n_layers, _, f2 = conv_w.shape
    pad = jnp.zeros((n_layers, SUBLANES - CONV_WIDTH - 1, f2), F32)
    cw = jnp.concatenate([conv_w, conv_b[:, None, :], pad], axis=1)
    return w_up.astype(BF16), cw, w_down.astype(BF16)


def _mix_ffn(mix, w_out, x, g, ffn_params, layer, final_g, tm, final_norm):
    b, s, d = x.shape
    w_up, cw, w_down = ffn_params
    f = w_down.shape[1]
    tf = FFN_TILE
    nf = f // tf
    ns = tf // LANES
    w_out = w_out.astype(BF16)
    const2 = lambda bi, si: (0, 0)
    tile = lambda w: pl.BlockSpec((None, tm, w), lambda bi, si: (bi, si, 0))
    resident = lambda a: pl.BlockSpec(a.shape, const2, pipeline_mode=pl.Buffered(1))
    stacked = lambda a, **kw: pl.BlockSpec((None,) + a.shape[1:], lambda bi, si: (layer, 0, 0), **kw)
    return pl.pallas_call(
        functools.partial(_ffn_kernel, n_mix=len(mix), tm=tm, tf=tf, nf=nf, ns=ns,
                          final_norm=final_norm),
        grid=(b, s // tm),
        in_specs=[tile(a.shape[-1]) for a in mix] + [
            resident(w_out),
            tile(d),
            pl.BlockSpec((1, d), const2),
            stacked(w_up, pipeline_mode=pl.Buffered(1)),
            stacked(cw),
            stacked(w_down, pipeline_mode=pl.Buffered(1)),
            pl.BlockSpec((1, d), const2),
        ],
        out_specs=tile(d),
        out_shape=jax.ShapeDtypeStruct((b, s, d), F32),
        scratch_shapes=[
            pltpu.VMEM((tm, d), BF16),
            pltpu.VMEM((tm, d), F32),
            pltpu.VMEM((2 * ns, tm + SUBLANES, LANES), F32),
            pltpu.VMEM((2 * ns, tm + SUBLANES, LANES), F32),
            pltpu.VMEM((nf, 2 * ns, SUBLANES, LANES), F32),
        ],
        compiler_params=pltpu.CompilerParams(
            dimension_semantics=("parallel", "arbitrary"), vmem_limit_bytes=VMEM_LIMIT),
    )(*mix, w_out, x, g.reshape(1, d), w_up, cw, w_down, final_g.reshape(1, d))


def _attn_kernel(*refs, split_qk, v_shared, tk, lam_init):
    if split_qk:
        q0_ref, q1_ref, k0_ref, k1_ref, v_ref = refs[:5]
        rest = refs[5:]
    else:
        q0_ref, k0_ref, v_ref = refs[:3]
        q1_ref, k1_ref = q0_ref, k0_ref
        rest = refs[3:]
    if v_shared:
        lam_ref, subln_ref, o_ref, acc0, acc1, mx0, mx1 = rest
    else:
        o_ref, acc0, acc1, mx0, mx1 = rest
    q_refs, k_refs, accs, mxs = (q0_ref, q1_ref), (k0_ref, k1_ref), (acc0, acc1), (mx0, mx1)
    acc_rep = acc0.shape[1] // LANES
    s_len = q0_ref.shape[0]
    lo = _lo_mask()

    def k_of(h, start):
        kb = k_refs[h][start:start + tk, :]
        if split_qk:
            return kb
        z = jnp.zeros_like(kb)
        return jnp.where(lo, kb, z) if h == 0 else jnp.where(lo, z, kb)

    def v_of(h, start):
        vb = v_ref[start:start + tk, :]
        ones = jnp.ones_like(vb)
        if v_shared:
            return jnp.concatenate([vb, ones], axis=1)
        return jnp.where(lo, vb, ones) if h == 0 else jnp.where(lo, ones, vb)

    def update(h, start):
        rows = s_len - start
        s = lax.dot_general(q_refs[h][start:s_len, :], k_of(h, start), NT_DIMS,
                            preferred_element_type=F32)
        row = lax.broadcasted_iota(jnp.int32, (rows, tk), 0)
        col = lax.broadcasted_iota(jnp.int32, (rows, tk), 1)
        s = jnp.where(row >= col, s, -jnp.inf)
        m_old = mxs[h][start:s_len, :]
        m_new = jnp.maximum(m_old, jnp.max(s, axis=-1, keepdims=True))
        alpha = jnp.exp2(m_old - m_new)
        p = jnp.exp2(s - jnp.concatenate([m_new] * (tk // LANES), axis=1))
        if acc_rep > 1:
            alpha = jnp.concatenate([alpha] * acc_rep, axis=1)
        accs[h][start:s_len, :] = accs[h][start:s_len, :] * alpha + jnp.dot(
            p.astype(BF16), v_of(h, start), preferred_element_type=F32)
        mxs[h][start:s_len, :] = m_new

    acc0[...] = jnp.zeros_like(acc0)
    acc1[...] = jnp.zeros_like(acc1)
    mx0[...] = jnp.full(mx0.shape, -jnp.inf, F32)
    mx1[...] = jnp.full(mx1.shape, -jnp.inf, F32)
    for d in range(s_len // tk):
        for h in range(2):
            update(h, d * tk)
    a0 = acc0[...]
    a1 = acc1[...]
    if v_shared:
        o = a0[:, :LANES] / a0[:, LANES:] - lam_ref[...] * (a1[:, :LANES] / a1[:, LANES:])
        o = _rms(o, subln_ref[...]) * (1.0 - lam_init)
    else:
        o = jnp.where(lo, a0 / pltpu.roll(a0, HALF, 1), a1 / pltpu.roll(a1, HALF, 1))
    o_ref[...] = o.astype(o_ref.dtype)


def _pair_attention(ins, n_pairs, *, split_qk, v_shared, tk, out_width,
                    lam=None, subln=None, lam_init=0.0):
    b, s, _ = ins[0][0].shape
    in_specs = [pl.BlockSpec((None, s, LANES),
                             lambda bi, pi, base=base, step=step: (bi, 0, base + step * pi))
                for _, base, step in ins]
    args = [a for a, _, _ in ins]
    if v_shared:
        in_specs += [pl.BlockSpec((1, LANES), lambda bi, pi: (0, 0))] * 2
        args += [lam, subln]
    acc_w = 2 * LANES if v_shared else LANES
    return pl.pallas_call(
        functools.partial(_attn_kernel, split_qk=split_qk, v_shared=v_shared, tk=tk,
                          lam_init=lam_init),
        grid=(b, n_pairs),
        in_specs=in_specs,
        out_specs=pl.BlockSpec((None, s, LANES), lambda bi, pi: (bi, 0, pi)),
        out_shape=jax.ShapeDtypeStruct((b, s, out_width), BF16),
        scratch_shapes=[pltpu.VMEM((s, acc_w), F32), pltpu.VMEM((s, acc_w), F32),
                        pltpu.VMEM((s, LANES), F32), pltpu.VMEM((s, LANES), F32)],
        compiler_params=pltpu.CompilerParams(
            dimension_semantics=("parallel", "parallel"), vmem_limit_bytes=VMEM_LIMIT),
    )(*args)


def _count_bias(delta, patterns):
    cnt = jnp.zeros(delta.shape, F32)
    for window, dil in patterns:
        n_back = window // dil
        hit = (delta >= 0) & (delta <= n_back * dil) & (delta % dil == 0)
        cnt = cnt + hit.astype(F32)
    return jnp.where(cnt > 0, jnp.log2(jnp.maximum(cnt, 1.0)), -jnp.inf)


def _dilated_kernel(q_ref, k_ref, v_ref, bfull_ref, bdiag_ref, bsub_ref, o_ref,
                    acc2_0, acc2_1, m2_0, m2_1, qf, kf, vf, sa0, sa1, sb0, sb1, *, tq, dil, sub):
    s_len = q_ref.shape[0]
    nq = s_len // tq
    lo = _lo_mask()
    acc2, m2 = (acc2_0, acc2_1), (m2_0, m2_1)
    sbuf_a, sbuf_b = (sa0, sa1), (sb0, sb1)

    def k_of(kb, h):
        z = jnp.zeros_like(kb)
        return jnp.where(lo, kb, z) if h == 0 else jnp.where(lo, z, kb)

    def v_of(vb, h):
        ones = jnp.ones_like(vb)
        return jnp.where(lo, vb, ones) if h == 0 else jnp.where(lo, ones, vb)

    def softmax_pv(s, vb, h):
        m = jnp.max(s, axis=-1, keepdims=True)
        p = jnp.exp2(s - m)
        return m, jnp.dot(p.astype(BF16), v_of(vb, h), preferred_element_type=F32)

    qf[...] = q_ref[...].astype(F32)
    kf[...] = k_ref[...].astype(F32)
    vf[...] = v_ref[...].astype(F32)
    for r in range(dil):
        rows = pl.ds(r, sub, stride=dil)
        q = qf[rows, :].astype(BF16)
        kb = kf[rows, :].astype(BF16)
        vb = vf[rows, :].astype(BF16)
        for h in range(2):
            s = lax.dot_general(q, k_of(kb, h), NT_DIMS, preferred_element_type=F32)
            mp, ap = softmax_pv(s + bsub_ref[...], vb, h)
            acc2[h][rows, :] = ap
            m2[h][rows, :] = jnp.broadcast_to(mp, (sub, LANES))

    def scores(qi, sbuf):
        qs = pl.multiple_of(qi * tq, tq)
        q = q_ref[pl.ds(qs, tq), :]
        kb = k_ref[pl.ds(qs - tq, 2 * tq), :]
        for h in range(2):
            sbuf[h][...] = lax.dot_general(q, k_of(kb, h), NT_DIMS, preferred_element_type=F32)

    def finish(qs, klo, klen, s_of, bias_ref):
        vb = v_ref[pl.ds(klo, klen), :]
        merged = []
        for h in range(2):
            m, a = softmax_pv(s_of(h) + bias_ref[...], vb, h)
            mo = m2[h][pl.ds(qs, tq), :]
            mn = jnp.maximum(mo, m)
            merged.append(jnp.exp2(m - mn) * a + jnp.exp2(mo - mn) * acc2[h][pl.ds(qs, tq), :])
        a0, a1 = merged
        o = jnp.where(lo, a0 / pltpu.roll(a0, HALF, 1), a1 / pltpu.roll(a1, HALF, 1))
        o_ref[pl.ds(qs, tq), :] = o.astype(o_ref.dtype)

    def finish_tile(qi, sbuf):
        qs = pl.multiple_of(qi * tq, tq)
        finish(qs, qs - tq, 2 * tq, lambda h: sbuf[h][...], bfull_ref)

    q_first = q_ref[0:tq, :]
    k_first = k_ref[0:tq, :]
    finish(0, 0, tq, lambda h: lax.dot_general(q_first, k_of(k_first, h), NT_DIMS,
                                               preferred_element_type=F32), bdiag_ref)
    assert nq % 2 == 0
    scores(1, sbuf_a)

    def body(i, c):
        b = 2 * i + 1
        scores(b + 1, sbuf_b)
        finish_tile(b, sbuf_a)
        scores(b + 2, sbuf_a)
        finish_tile(b + 1, sbuf_b)
        return c

    lax.fori_loop(0, (nq - 2) // 2, body, 0)
    finish_tile(nq - 1, sbuf_a)


def _dilated_attention(pr, q_base, k_base, v_base, n_pairs, tq, out_width):
    b, s, _ = pr.shape
    dense_pat = tuple(p for p in DILATED_PATTERNS if p[0] <= tq)
    strided = tuple(p for p in DILATED_PATTERNS if p[0] > tq)
    assert len(strided) == 1 and s % strided[0][1] == 0
    dil = strided[0][1]
    sub = s // dil
    i = jnp.arange(tq, dtype=jnp.int32)[:, None]
    j = jnp.arange(2 * tq, dtype=jnp.int32)[None, :]
    bfull = _count_bias(i + tq - j, dense_pat)
    bdiag = bfull[:, tq:]
    si = jnp.arange(sub, dtype=jnp.int32)
    bsub = _count_bias((si[:, None] - si[None, :]) * dil, strided)
    seq = lambda base: pl.BlockSpec((None, s, LANES), lambda bi, pi, base=base: (bi, 0, base + pi))
    const = lambda a: pl.BlockSpec(a.shape, lambda bi, pi: (0, 0))
    return pl.pallas_call(
        functools.partial(_dilated_kernel, tq=tq, dil=dil, sub=sub),
        grid=(b, n_pairs),
        in_specs=[seq(q_base), seq(k_base), seq(v_base), const(bfull), const(bdiag), const(bsub)],
        out_specs=pl.BlockSpec((None, s, LANES), lambda bi, pi: (bi, 0, pi)),
        out_shape=jax.ShapeDtypeStruct((b, s, out_width), BF16),
        scratch_shapes=[pltpu.VMEM((s, LANES), F32)] * 7 + [pltpu.VMEM((tq, 2 * tq), F32)] * 4,
        compiler_params=pltpu.CompilerParams(
            dimension_semantics=("parallel", "parallel"), vmem_limit_bytes=VMEM_LIMIT),
    )(pr, pr, pr, bfull, bdiag, bsub)


def _hybrid_mixer(x, layer_idx, pos, g, w_in, lq1, lk1, lq2, lk2, subln, tiles):
    b, s, d = x.shape
    x2 = x.reshape(b * s, d)
    pr = _hyb_proj(x2, g, w_in.astype(BF16), pos, tiles["tm"]).reshape(b, s, HYB_IN_WIDTH)
    cb = A_WIDTH // LANES
    out_a = _dilated_attention(pr, 0, cb, 2 * cb, A_HEADS // 2, tiles["tdil"], A_WIDTH)
    lam_init = 0.8 - 0.6 * math.exp(-0.3 * layer_idx)
    lam = jnp.exp(jnp.sum(lq1 * lk1)) - jnp.exp(jnp.sum(lq2 * lk2)) + lam_init
    out_b = _pair_attention([(pr, 3 * cb, 1), (pr, 4 * cb, 1), (pr, 5 * cb, 1)],
                            B_HEADS, split_qk=False, v_shared=True, tk=tiles["tk"],
                            out_width=B_V_WIDTH, lam=jnp.full((1, LANES), lam, F32),
                            subln=subln.reshape(1, LANES), lam_init=lam_init)
    return [out_a, out_b]


def _mla_mixer(x, pos, g, w_in, q_norm, w_uq, kv_norm, w_ukv, tiles):
    b, s, d = x.shape
    x2 = x.reshape(b * s, d)
    qf, kf, vf = _mla_proj(x2, g, w_in, q_norm, w_uq, kv_norm, w_ukv, pos, tiles["tm"])
    qf = qf.reshape(b, s, -1)
    kf = kf.reshape(b, s, -1)
    vf = vf.reshape(b, s, -1)
    o = _pair_attention([(qf, 0, 2), (qf, 1, 2), (kf, 0, 2), (kf, 1, 2), (vf, 0, 1)],
                        MLA_HEADS // 2, split_qk=True, v_shared=False, tk=tiles["tk"],
                        out_width=MLA_HEADS * MLA_V_DIM)
    return [o]


def _choose_tiles(s):
    return dict(tm=min(1024, s), tf=min(1024, s), tk=min(512, s), tdil=min(512, s))


def kernel(x, positions, attn_norm, ffn_norm, final_norm, hyb_w_in, hyb_w_out, diff_lambda_q1, diff_lambda_k1, diff_lambda_q2, diff_lambda_k2, diff_subln, mla_w_in, mla_q_norm, mla_w_uq, mla_kv_norm, mla_w_ukv, mla_w_out, ffn_w_up, ffn_conv_w, ffn_conv_b, ffn_w_down):
    depth = attn_norm.shape[0]
    tiles = _choose_tiles(x.shape[1])
    pos = positions.astype(F32).reshape(1, -1)
    ffn_params = _ffn_params(ffn_w_up, ffn_conv_w, ffn_conv_b, ffn_w_down)
    for i in range(depth):
        j = i // 2
        if i % 2 == 0:
            mix = _hybrid_mixer(x, i, pos, attn_norm[i], hyb_w_in[j], diff_lambda_q1[j],
                                diff_lambda_k1[j], diff_lambda_q2[j], diff_lambda_k2[j],
                                diff_subln[j], tiles)
            w_out = hyb_w_out[j]
        else:
            mix = _mla_mixer(x, pos, attn_norm[i], mla_w_in[j], mla_q_norm[j], mla_w_uq[j],
                             mla_kv_norm[j], mla_w_ukv[j], tiles)
            w_out = mla_w_out[j]
        x = _mix_ffn(mix, w_out, x, ffn_norm[i], ffn_params, i, final_norm, tiles["tf"],
                     final_norm=(i == depth - 1))
    return x
```
